```python
import jax, jax.numpy as jnp
from jax import lax
import numpy as np

D_MODEL = 1024
BATCH = 4
SEQ = 8192
DEPTH = 4

CTX_LEN = 256
GRID_W = 64
NORM_EPS = 1e-6
NEG_INF = -1e30

POOL_WINDOWS = (2, 4, 8, 16)
POOL_GROUPS = len(POOL_WINDOWS)
POOL_GROUP_DIM = D_MODEL // 16
POOL_DIM = POOL_GROUPS * POOL_GROUP_DIM

HEAD_DIM = 64
ATTN_HEADS = D_MODEL // 128
ATTN_KV_HEADS = ATTN_HEADS // 4
Q_PER_KV = ATTN_HEADS // ATTN_KV_HEADS
ATTN_DIM = ATTN_HEADS * HEAD_DIM
KV_DIM = ATTN_KV_HEADS * HEAD_DIM
ATTN_WINDOW = 128
ATTN_BLOCK = 128
ROPE_BASE = 10000.0
ROPE_AXIS_DIM = HEAD_DIM // 2

HG_HEADS = 4
HG_DIM = 64
HG_WIDTH = HG_HEADS * HG_DIM
HG_CHUNK = 64

MIX_WIDTH = POOL_DIM + ATTN_DIM + HG_WIDTH
IN_SIZES = (POOL_DIM, ATTN_DIM, KV_DIM, KV_DIM, HG_WIDTH, HG_WIDTH, HG_WIDTH, HG_WIDTH, HG_WIDTH)
IN_SPLITS = tuple(sum(IN_SIZES[:i + 1]) for i in range(len(IN_SIZES) - 1))
IN_COLS = sum(IN_SIZES)

N_GROUPS = 4
EXPERTS_PER_GROUP = 8
N_EXPERTS = N_GROUPS * EXPERTS_PER_GROUP
TOP_K = 2
EXPERT_HIDDEN = D_MODEL // 2
MOE_BLOCK = 256

kernel_name = 'hybrid_pool_swa_hgrn2_hmoe_dit'


def rms_norm(t, g):
    tf = t.astype(jnp.float32)
    y = tf * lax.rsqrt(jnp.mean(tf * tf, axis=-1, keepdims=True) + NORM_EPS)
    return (y * g.astype(jnp.float32)).astype(t.dtype)


def modulate(t, g, shift, scale):
    return rms_norm(t, g) * (1.0 + scale) + shift


def to_heads(t, n):
    return t.reshape(t.shape[0], t.shape[1], n, t.shape[-1] // n)


def axial_rope(n_rows):
    row = jnp.repeat(jnp.arange(n_rows, dtype=jnp.float32), GRID_W)
    col = jnp.tile(jnp.arange(GRID_W, dtype=jnp.float32), n_rows)
    inv = ROPE_BASE ** (-jnp.arange(0, ROPE_AXIS_DIM, 2, dtype=jnp.float32) / ROPE_AXIS_DIM)
    ang = jnp.concatenate([row[:, None] * inv, col[:, None] * inv], axis=-1)
    return jnp.cos(ang), jnp.sin(ang)


def apply_rope(t, cos, sin):
    tf = t.astype(jnp.float32)
    half = HEAD_DIM // 2
    t1, t2 = tf[..., :half], tf[..., half:]
    c, s = cos[:, None, :], sin[:, None, :]
    return jnp.concatenate([t1 * c - t2 * s, t2 * c + t1 * s], axis=-1).astype(t.dtype)


def pool_mixer(u, pool_w, pool_scale):
    B_, L, _ = u.shape
    uf = u.astype(jnp.float32)
    cs = jnp.concatenate([jnp.zeros((B_, 1, POOL_DIM), jnp.float32), jnp.cumsum(uf, axis=1)], axis=1)
    t = jnp.arange(L)
    means = []
    for gi, w in enumerate(POOL_WINDOWS):
        lo = jnp.clip(t - (w - 1) // 2, 0, L)
        hi = jnp.clip(t + w // 2 + 1, 0, L)
        csg = cs[..., gi * POOL_GROUP_DIM:(gi + 1) * POOL_GROUP_DIM]
        means.append((csg[:, hi] - csg[:, lo]) / (hi - lo).astype(jnp.float32)[None, :, None])
    pooled = jnp.concatenate(means, axis=-1) - uf
    y = jnp.einsum('blgc,gcd->blgd', pooled.reshape(B_, L, POOL_GROUPS, POOL_GROUP_DIM),
                   pool_w.astype(jnp.float32))
    return y.reshape(B_, L, POOL_DIM) * pool_scale.astype(jnp.float32)


def windowed_sink_attention(q, k, v, kc, vc, sink):
    B_, L = q.shape[:2]
    nb = L // ATTN_BLOCK
    scale = HEAD_DIM ** -0.5
    qb = q.reshape(B_, nb, ATTN_BLOCK, ATTN_KV_HEADS, Q_PER_KV, HEAD_DIM)

    def band_blocks(t):
        tp = jnp.pad(t, ((0, 0), (ATTN_BLOCK, ATTN_BLOCK), (0, 0), (0, 0)))
        tp = tp.reshape(B_, nb + 2, ATTN_BLOCK, ATTN_KV_HEADS, HEAD_DIM)
        return jnp.concatenate([tp[:, :-2], tp[:, 1:-1], tp[:, 2:]], axis=2)

    kb, vb = band_blocks(k), band_blocks(v)
    q_pos = jnp.arange(L).reshape(nb, ATTN_BLOCK)
    k_pos = jnp.arange(nb)[:, None] * ATTN_BLOCK - ATTN_BLOCK + jnp.arange(3 * ATTN_BLOCK)[None, :]
    band = ((jnp.abs(q_pos[:, :, None] - k_pos[:, None, :]) <= ATTN_WINDOW)
            & (k_pos[:, None, :] >= 0) & (k_pos[:, None, :] < L))
    s_loc = jnp.einsum('bnqkgd,bnskd->bkgnqs', qb, kb).astype(jnp.float32) * scale
    s_loc = jnp.where(band, s_loc, NEG_INF)
    s_ctx = jnp.einsum('bnqkgd,bckd->bkgnqc', qb, kc).astype(jnp.float32) * scale
    sink_l = jnp.broadcast_to(sink.astype(jnp.float32).reshape(1, ATTN_KV_HEADS, Q_PER_KV, 1, 1, 1),
                              s_loc.shape[:-1] + (1,))
    p = jax.nn.softmax(jnp.concatenate([s_loc, s_ctx, sink_l], axis=-1), axis=-1)
    n_loc = 3 * ATTN_BLOCK
    n_ctx = kc.shape[1]
    p_loc = p[..., :n_loc].astype(v.dtype)
    p_ctx = p[..., n_loc:n_loc + n_ctx].astype(v.dtype)
    o = (jnp.einsum('bkgnqs,bnskd->bnqkgd', p_loc, vb)
         + jnp.einsum('bkgnqc,bckd->bnqkgd', p_ctx, vc))
    return o.reshape(B_, L, ATTN_DIM)


def context_attention(qc, kc, vc, sink):
    B_, C = qc.shape[:2]
    qb = qc.reshape(B_, C, ATTN_KV_HEADS, Q_PER_KV, HEAD_DIM)
    s = jnp.einsum('bqkgd,bckd->bkgqc', qb, kc).astype(jnp.float32) * (HEAD_DIM ** -0.5)
    sink_c = jnp.broadcast_to(sink.astype(jnp.float32).reshape(1, ATTN_KV_HEADS, Q_PER_KV, 1, 1),
                              s.shape[:-1] + (1,))
    p = jax.nn.softmax(jnp.concatenate([s, sink_c], axis=-1), axis=-1)[..., :C]
    o = jnp.einsum('bkgqc,bckd->bqkgd', p.astype(vc.dtype), vc)
    return o.reshape(B_, C, ATTN_DIM)


def gla_chunk_scan(q, k, v, log_f, s0):
    B_, L, H, _ = q.shape
    dv = v.shape[-1]
    nc = L // HG_CHUNK

    def chunks(t):
        return t.astype(jnp.float32).reshape(B_, nc, HG_CHUNK, H, t.shape[-1]).transpose(1, 0, 3, 2, 4)

    tri = jnp.tril(jnp.ones((HG_CHUNK, HG_CHUNK), bool))[:, :, None]

    def step(S, inp):
        qi, ki, vi, gi = inp
        b = jnp.cumsum(gi, axis=2)
        diff = b[:, :, :, None, :] - b[:, :, None, :, :]
        decay = jnp.exp(jnp.where(tri, diff, NEG_INF))
        scores = jnp.einsum('bhtd,bhsd,bhtsd->bhts', qi, ki, decay)
        o = (jnp.einsum('bhts,bhsv->bhtv', scores, vi)
             + jnp.einsum('bhtd,bhdv->bhtv', qi * jnp.exp(b), S))
        b_end = b[:, :, -1:, :]
        S_new = (jnp.exp(b_end[:, :, 0, :])[..., None] * S
                 + jnp.einsum('bhsd,bhsv->bhdv', ki * jnp.exp(b_end - b), vi))
        return S_new, o

    s_fin, oc = lax.scan(step, s0.astype(jnp.float32), (chunks(q), chunks(k), chunks(v), chunks(log_f)))
    return oc.transpose(1, 0, 3, 2, 4).reshape(B_, L, H, dv), s_fin


def hgrn_forget(z, lb):
    f = lb + (1.0 - lb) * jax.nn.sigmoid(z)
    return jnp.log(f), (1.0 - lb) * jax.nn.sigmoid(-z)


def hgrn_prepare(parts, lb_f, lb_b):
    hq, hi, hzf, hzb, hgate = parts
    heads = lambda t: to_heads(t.astype(jnp.float32), HG_HEADS)
    q = jax.nn.silu(heads(hq))
    v = heads(hi)
    gf, kf = hgrn_forget(heads(hzf), lb_f.reshape(HG_HEADS, HG_DIM))
    gb, kb = hgrn_forget(heads(hzb), lb_b.reshape(HG_HEADS, HG_DIM))
    return q, v, gf, kf, gb, kb, heads(hgate)


def hgrn_readout(o, gate, norm_g):
    B_, L = o.shape[:2]
    return (rms_norm(o, norm_g) * jax.nn.silu(gate)).reshape(B_, L, HG_WIDTH)


def hgrn_mixer(lat_parts, ctx_parts, lb_f, lb_b, norm_g, ctx_out):
    q, v, gf, kf, gb, kb, gate = hgrn_prepare(lat_parts, lb_f, lb_b)
    qc, vc, gfc, kfc, gbc, kbc, gatec = hgrn_prepare(ctx_parts, lb_f, lb_b)
    rev = lambda t: jnp.flip(t, axis=1)
    s0 = jnp.zeros((q.shape[0], HG_HEADS, HG_DIM, HG_DIM), jnp.float32)
    oc_f, sc_f = gla_chunk_scan(qc, kfc, vc, gfc, s0)
    oc_b, sc_b = gla_chunk_scan(rev(qc), rev(kbc), rev(vc), rev(gbc), s0)
    o_f, _ = gla_chunk_scan(q, kf, v, gf, sc_f)
    o_b, _ = gla_chunk_scan(rev(q), rev(kb), rev(v), rev(gb), sc_b)
    out = hgrn_readout(o_f + rev(o_b), gate, norm_g)
    out_c = hgrn_readout(oc_f + rev(oc_b), gatec, norm_g) if ctx_out else None
    return out, out_c


def token_mixers(p, pc, pool_w, pool_scale, q_norm_g, k_norm_g, sink, lb_f, lb_b, hg_norm_g,
                 cos, sin, ctx_out):
    u, q, k, v, hq, hi, hzf, hzb, hgate = jnp.split(p, IN_SPLITS, axis=-1)
    uc, qc, kc, vc, hqc, hic, hzfc, hzbc, hgatec = jnp.split(pc, IN_SPLITS, axis=-1)
    dt = p.dtype
    a = pool_mixer(u, pool_w, pool_scale)
    q = apply_rope(rms_norm(to_heads(q, ATTN_HEADS), q_norm_g), cos, sin)
    k = apply_rope(rms_norm(to_heads(k, ATTN_KV_HEADS), k_norm_g), cos, sin)
    v = to_heads(v, ATTN_KV_HEADS)
    kc = rms_norm(to_heads(kc, ATTN_KV_HEADS), k_norm_g)
    vc = to_heads(vc, ATTN_KV_HEADS)
    b = windowed_sink_attention(q, k, v, kc, vc, sink)
    c_lat, c_ctx = hgrn_mixer((hq, hi, hzf, hzb, hgate), (hqc, hic, hzfc, hzbc, hgatec),
                              lb_f, lb_b, hg_norm_g, ctx_out)
    mix = jnp.concatenate([a.astype(dt), b.astype(dt), c_lat.astype(dt)], axis=-1)
    if not ctx_out:
        return mix, None
    ac = pool_mixer(uc, pool_w, pool_scale)
    bc = context_attention(rms_norm(to_heads(qc, ATTN_HEADS), q_norm_g), kc, vc, sink)
    mixc = jnp.concatenate([ac.astype(dt), bc.astype(dt), c_ctx.astype(dt)], axis=-1)
    return mix, mixc


def expert_ffn(h, expert_idx, gate_w, w_gate, w_up, w_down):
    T, D = h.shape
    A = T * TOP_K
    flat_e = expert_idx.reshape(-1)
    order = jnp.argsort(flat_e)
    sorted_e = flat_e[order]
    counts = jnp.bincount(flat_e, length=N_EXPERTS)
    padded = ((counts + MOE_BLOCK - 1) // MOE_BLOCK) * MOE_BLOCK
    padded_end = jnp.cumsum(padded)
    padded_start = padded_end - padded
    start = jnp.cumsum(counts) - counts
    dest = padded_start[sorted_e] + (jnp.arange(A) - start[sorted_e])
    n_blocks = -(-A // MOE_BLOCK) + N_EXPERTS
    P = n_blocks * MOE_BLOCK
    token_sorted = (order // TOP_K).astype(jnp.int32)
    token_of_slot = jnp.zeros((P,), jnp.int32).at[dest].set(token_sorted)
    block_expert = jnp.minimum(
        jnp.searchsorted(padded_end, jnp.arange(n_blocks) * MOE_BLOCK, side='right'), N_EXPERTS - 1)
    xs = h[token_of_slot].reshape(n_blocks, MOE_BLOCK, D)

    def run(args):
        xb, e = args
        hid = jax.nn.silu(xb @ w_gate[e]) * (xb @ w_up[e])
        return hid @ w_down[e]

    ys = lax.map(run, (xs, block_expert)).reshape(P, D)
    contrib = ys[dest].astype(jnp.float32) * gate_w.reshape(-1)[order][:, None]
    return jax.ops.segment_sum(contrib, token_sorted, num_segments=T).astype(h.dtype)


def hier_moe(h, wg, bg, we, be, w_gate, w_up, w_down):
    T = h.shape[0]
    lg = (h @ wg + bg).astype(jnp.float32)
    pg = jax.nn.softmax(lg, axis=-1)
    _, g_idx = lax.top_k(lg, 1)
    p_grp = jnp.take_along_axis(pg, g_idx, axis=-1)
    le = (h @ we + be).astype(jnp.float32).reshape(T, N_GROUPS, EXPERTS_PER_GROUP)
    le = jnp.take_along_axis(le, g_idx[:, :, None], axis=1)[:, 0]
    pe = jax.nn.softmax(le, axis=-1)
    w_top, e_top = lax.top_k(pe, TOP_K)
    w_top = w_top / jnp.sum(w_top, axis=-1, keepdims=True)
    expert_idx = g_idx * EXPERTS_PER_GROUP + e_top
    return expert_ffn(h, expert_idx, p_grp * w_top, w_gate, w_up, w_down)


def setup_inputs(seed: int = 0) -> dict:
    key = jax.random.key(seed)
    ks = jax.random.split(key, 24)
    f32 = jnp.float32
    d = D_MODEL

    def nrm(k, shape, s):
        return jax.random.normal(k, shape, f32) * s

    return {
        'x': nrm(ks[0], (BATCH, SEQ, d), 1.0),
        'c': nrm(ks[1], (BATCH, d), 1.0),
        'ctx': nrm(ks[2], (BATCH, CTX_LEN, d), 1.0),
        'c_ctx': nrm(ks[3], (d,), 1.0),
        'norm1_g': 1.0 + nrm(ks[4], (DEPTH, d), 0.05),
        'norm2_g': 1.0 + nrm(ks[5], (DEPTH, d), 0.05),
        'ada_w': nrm(ks[6], (DEPTH, d, 6 * d), 0.5 * d ** -0.5),
        'ada_b': nrm(ks[7], (DEPTH, 6 * d), 0.02),
        'w_in': nrm(ks[8], (DEPTH, d, IN_COLS), d ** -0.5),
        'w_out': nrm(ks[9], (DEPTH, MIX_WIDTH, d), MIX_WIDTH ** -0.5),
        'pool_w': nrm(ks[10], (DEPTH, POOL_GROUPS, POOL_GROUP_DIM, POOL_GROUP_DIM), POOL_GROUP_DIM ** -0.5),
        'pool_scale': 1.0 + nrm(ks[11], (DEPTH, POOL_DIM), 0.1),
        'q_norm_g': 1.0 + nrm(ks[12], (DEPTH, HEAD_DIM), 0.05),
        'k_norm_g': 1.0 + nrm(ks[13], (DEPTH, HEAD_DIM), 0.05),
        'attn_sink': nrm(ks[14], (DEPTH, ATTN_HEADS), 0.5),
        'hgrn_lb': nrm(ks[15], (DEPTH, 2, HG_WIDTH), 0.5),
        'hgrn_norm_g': 1.0 + nrm(ks[16], (DEPTH, HG_DIM), 0.05),
        'router_group_w': nrm(ks[17], (DEPTH, d, N_GROUPS), d ** -0.5),
        'router_group_b': nrm(ks[18], (DEPTH, N_GROUPS), 0.01),
        'router_expert_w': nrm(ks[19], (DEPTH, d, N_EXPERTS), d ** -0.5),
        'router_expert_b': nrm(ks[20], (DEPTH, N_EXPERTS), 0.01),
        'expert_w_gate': nrm(ks[21], (DEPTH, N_EXPERTS, d, EXPERT_HIDDEN), d ** -0.5),
        'expert_w_up': nrm(ks[22], (DEPTH, N_EXPERTS, d, EXPERT_HIDDEN), d ** -0.5),
        'expert_w_down': nrm(ks[23], (DEPTH, N_EXPERTS, EXPERT_HIDDEN, d), EXPERT_HIDDEN ** -0.5),
    }


def reference(x, c, ctx, c_ctx, norm1_g, norm2_g, ada_w, ada_b, w_in, w_out, pool_w, pool_scale,
              q_norm_g, k_norm_g, attn_sink, hgrn_lb, hgrn_norm_g, router_group_w, router_group_b,
              router_expert_w, router_expert_b, expert_w_gate, expert_w_up, expert_w_down):
    B_, L, _ = x.shape
    C = ctx.shape[1]
    n_rows = L // GRID_W
    cos, sin = axial_rope(n_rows)
    sm = jax.nn.softmax(hgrn_lb.astype(jnp.float32), axis=0)
    lb_all = jnp.cumsum(sm, axis=0) - sm[0]
    s_lat = jax.nn.silu(c)
    s_ctx = jax.nn.silu(c_ctx)
    for l in range(DEPTH):
        ctx_out = l < DEPTH - 1
        mod = (s_lat @ ada_w[l] + ada_b[l])[:, None, :]
        modc = s_ctx @ ada_w[l] + ada_b[l]
        sh1, sc1, g1, sh2, sc2, g2 = jnp.split(mod, 6, axis=-1)
        csh1, csc1, cg1, csh2, csc2, cg2 = jnp.split(modc, 6, axis=-1)
        h = modulate(x, norm1_g[l], sh1, sc1)
        hc = modulate(ctx, norm1_g[l], csh1, csc1)
        mix, mixc = token_mixers(h @ w_in[l], hc @ w_in[l], pool_w[l], pool_scale[l], q_norm_g[l],
                                 k_norm_g[l], attn_sink[l], lb_all[l, 0], lb_all[l, 1], hgrn_norm_g[l],
                                 cos, sin, ctx_out)
        x = x + g1 * (mix @ w_out[l])
        h2 = modulate(x, norm2_g[l], sh2, sc2).reshape(B_ * L, D_MODEL)
        moe_args = (router_group_w[l], router_group_b[l], router_expert_w[l], router_expert_b[l],
                    expert_w_gate[l], expert_w_up[l], expert_w_down[l])
        if ctx_out:
            ctx = ctx + cg1 * (mixc @ w_out[l])
            h2c = modulate(ctx, norm2_g[l], csh2, csc2).reshape(B_ * C, D_MODEL)
            y = hier_moe(jnp.concatenate([h2, h2c], axis=0), *moe_args)
            x = x + g2 * y[:B_ * L].reshape(B_, L, D_MODEL)
            ctx = ctx + cg2 * y[B_ * L:].reshape(B_, C, D_MODEL)
        else:
            y = hier_moe(h2, *moe_args)
            x = x + g2 * y.reshape(B_, L, D_MODEL)
    return x
```

```python
import functools

import numpy as np
import jax
import jax.numpy as jnp
from jax import lax
from jax.experimental import pallas as pl
from jax.experimental.pallas import tpu as pltpu

D_MODEL = 1024
DEPTH = 4
GRID_W = 64
NORM_EPS = 1e-6
NEG_INF = -1e30

POOL_WINDOWS = (2, 4, 8, 16)
POOL_GROUP_DIM = 64
POOL_DIM = 256

HEAD_DIM = 64
ATTN_HEADS = 8
ATTN_KV_HEADS = 2
Q_PER_KV = 4
ATTN_DIM = 512
KV_DIM = 128
ATTN_WINDOW = 128
ATTN_BLOCK = 128
ROPE_BASE = 10000.0
ROPE_AXIS_DIM = 32

HG_HEADS = 4
HG_DIM = 64
HG_WIDTH = 256
HG_CHUNK = 64
HG_LEVELS = 7

MIX_WIDTH = 1024
IN_SIZES = (POOL_DIM, ATTN_DIM, KV_DIM, KV_DIM, HG_WIDTH, HG_WIDTH, HG_WIDTH, HG_WIDTH, HG_WIDTH)
IN_OFFS = tuple(int(sum(IN_SIZES[:i])) for i in range(len(IN_SIZES) + 1))
IN_COLS = IN_OFFS[-1]

N_GROUPS = 4
EXPERTS_PER_GROUP = 8
N_EXPERTS = 32
TOP_K = 2
EXPERT_HIDDEN = 512
MOE_BLOCK = 256
ROUTER_COLS = 128

ROW_TILE = 256
ADA_COL_TILE = 1536
VMEM_LIMIT = 56 * 1024 * 1024

F32 = jnp.float32
BF16 = jnp.bfloat16


def _cparams(*sem):
    return pltpu.CompilerParams(dimension_semantics=sem, vmem_limit_bytes=VMEM_LIMIT)


def _rms_scale(x):
    return lax.rsqrt(jnp.mean(x * x, axis=-1, keepdims=True) + NORM_EPS)


def _ada_kernel(c_ref, w_ref, b_ref, o_ref):
    s = c_ref[...]
    s = (s * jax.nn.sigmoid(s)).astype(BF16)
    o_ref[0] = jnp.dot(s, w_ref[0].astype(BF16), preferred_element_type=F32) + b_ref[0]


def _ada(craw, ada_w, ada_b):
    depth = ada_w.shape[0]
    ncol = ada_w.shape[2]
    return pl.pallas_call(
        _ada_kernel,
        out_shape=jax.ShapeDtypeStruct((depth, 8, ncol), F32),
        grid=(depth, ncol // ADA_COL_TILE),
        in_specs=[
            pl.BlockSpec((8, D_MODEL), lambda l, j: (0, 0)),
            pl.BlockSpec((1, D_MODEL, ADA_COL_TILE), lambda l, j: (l, 0, j)),
            pl.BlockSpec((1, 1, ADA_COL_TILE), lambda l, j: (l, 0, j)),
        ],
        out_specs=pl.BlockSpec((1, 8, ADA_COL_TILE), lambda l, j: (l, 0, j)),
        compiler_params=_cparams("arbitrary", "arbitrary"),
        name="ada_mod",
    )(craw, ada_w, ada_b.reshape(depth, 1, ncol))


def _in_proj_kernel(x_ref, g_ref, mod_ref, w_ref, *out_refs):
    x = x_ref[0]
    mod = mod_ref[0]
    sh = mod[:, 0:D_MODEL]
    sc = mod[:, D_MODEL:2 * D_MODEL]
    h = (x * _rms_scale(x) * g_ref[...] * (1.0 + sc) + sh).astype(BF16)
    for o_ref, lo, hi in zip(out_refs, IN_OFFS[:-1], IN_OFFS[1:]):
        o_ref[0] = jnp.dot(h, w_ref[:, lo:hi], preferred_element_type=F32)


def _mod_spec(n_batch):
    return pl.BlockSpec((1, 1, 6 * D_MODEL), lambda b, i: (jnp.where(i == 0, n_batch, b), 0, 0))


def _in_proj(xs, g1, mod, w_in_bf16):
    B, N, _ = xs.shape
    nt = N // ROW_TILE
    return pl.pallas_call(
        _in_proj_kernel,
        out_shape=[jax.ShapeDtypeStruct((B, N, w), F32) for w in IN_SIZES],
        grid=(B, nt),
        in_specs=[
            pl.BlockSpec((1, ROW_TILE, D_MODEL), lambda b, i: (b, i, 0)),
            pl.BlockSpec((1, D_MODEL), lambda b, i: (0, 0)),
            _mod_spec(B),
            pl.BlockSpec((D_MODEL, IN_COLS), lambda b, i: (0, 0)),
        ],
        out_specs=[pl.BlockSpec((1, ROW_TILE, w), lambda b, i: (b, i, 0)) for w in IN_SIZES],
        compiler_params=_cparams("arbitrary", "arbitrary"),
        name="in_proj",
    )(xs, g1.reshape(1, D_MODEL), mod, w_in_bf16)


def _qk_prep_kernel(q_ref, k_ref, v_ref, cos_ref, sin_ref, gq_ref, gk_ref, qh_ref, kh_ref, vh_ref):
    cos = cos_ref[...]
    sin = sin_ref[...]
    lane = lax.broadcasted_iota(jnp.int32, (1, 2 * HEAD_DIM), 1)
    lo_head = lane < HEAD_DIM
    first_half = (lane % HEAD_DIM) < (HEAD_DIM // 2)

    def norm_rope(x2, g, scale):
        sq = x2 * x2
        s0 = jnp.sum(jnp.where(lo_head, sq, 0.0), axis=-1, keepdims=True)
        s1 = jnp.sum(jnp.where(lo_head, 0.0, sq), axis=-1, keepdims=True)
        ms = jnp.where(lo_head, s0, s1) * (1.0 / HEAD_DIM)
        xn = x2 * lax.rsqrt(ms + NORM_EPS) * g
        swapped = jnp.where(first_half, pltpu.roll(xn, 2 * HEAD_DIM - HEAD_DIM // 2, 1),
                            pltpu.roll(xn, HEAD_DIM // 2, 1))
        return (xn * cos + swapped * sin) * scale

    for p in range(ATTN_HEADS // 2):
        y = norm_rope(q_ref[0, :, 2 * HEAD_DIM * p:2 * HEAD_DIM * (p + 1)], gq_ref[...], HEAD_DIM ** -0.5)
        qh_ref[0, 2 * p] = y[:, :HEAD_DIM].astype(BF16)
        qh_ref[0, 2 * p + 1] = y[:, HEAD_DIM:].astype(BF16)
    y = norm_rope(k_ref[0], gk_ref[...], 1.0)
    kh_ref[0, 0] = y[:, :HEAD_DIM].astype(BF16)
    kh_ref[0, 1] = y[:, HEAD_DIM:].astype(BF16)
    v = v_ref[0]
    vh_ref[0, 0] = v[:, :HEAD_DIM].astype(BF16)
    vh_ref[0, 1] = v[:, HEAD_DIM:].astype(BF16)


def _qk_prep(q, k, v, cos2, sin2, gq2, gk2):
    B, N, _ = q.shape
    nt = N // ROW_TILE
    row = lambda w: pl.BlockSpec((1, ROW_TILE, w), lambda b, i: (b, i, 0))
    tab = pl.BlockSpec((ROW_TILE, 2 * HEAD_DIM), lambda b, i: (i, 0))
    vec = pl.BlockSpec((1, 2 * HEAD_DIM), lambda b, i: (0, 0))
    head = lambda n: pl.BlockSpec((1, n, ROW_TILE, HEAD_DIM), lambda b, i: (b, 0, i, 0))
    return pl.pallas_call(
        _qk_prep_kernel,
        out_shape=[jax.ShapeDtypeStruct((B, ATTN_HEADS, N, HEAD_DIM), BF16),
                   jax.ShapeDtypeStruct((B, ATTN_KV_HEADS, N, HEAD_DIM), BF16),
                   jax.ShapeDtypeStruct((B, ATTN_KV_HEADS, N, HEAD_DIM), BF16)],
        grid=(B, nt),
        in_specs=[row(ATTN_DIM), row(KV_DIM), row(KV_DIM), tab, tab, vec, vec],
        out_specs=[head(ATTN_HEADS), head(ATTN_KV_HEADS), head(ATTN_KV_HEADS)],
        compiler_params=_cparams("arbitrary", "arbitrary"),
        name="qk_prep",
    )(q, k, v, cos2, sin2, gq2, gk2)


def _attn_kernel(sink_ref, q_ref, kp_ref, kc_ref, kn_ref, kx_ref, vp_ref, vc_ref, vn_ref, vx_ref,
                 o_ref, *, n_ctx, n_rows):
    i = pl.program_id(1)
    blk = ATTN_BLOCK
    t = i * blk + lax.broadcasted_iota(jnp.int32, (blk, 1), 0)
    kr = (i - 1) * blk + lax.broadcasted_iota(jnp.int32, (1, 3 * blk), 1)
    ok_loc = (t >= n_ctx) & (kr >= n_ctx) & (kr < n_rows) & (jnp.abs(t - kr) <= ATTN_WINDOW)
    bias = jnp.concatenate([jnp.where(ok_loc, 0.0, NEG_INF), jnp.zeros((blk, n_ctx), F32)],
                           axis=1)[None]
    heads = []
    for g in range(ATTN_KV_HEADS):
        q = q_ref[0, Q_PER_KV * g:Q_PER_KV * (g + 1)].reshape(Q_PER_KV * blk, HEAD_DIM)
        k = jnp.concatenate([kp_ref[0, g], kc_ref[0, g], kn_ref[0, g], kx_ref[0, g]], axis=0)
        v = jnp.concatenate([vp_ref[0, g], vc_ref[0, g], vn_ref[0, g], vx_ref[0, g]], axis=0)
        s = lax.dot_general(q, k, (((1,), (1,)), ((), ())), preferred_element_type=F32)
        nk = s.shape[-1]
        s = (s.reshape(Q_PER_KV, blk, nk) + bias).reshape(Q_PER_KV * blk, nk)
        sk = jnp.concatenate(
            [jnp.full((blk, 1), sink_ref[Q_PER_KV * g + hh], F32) for hh in range(Q_PER_KV)], axis=0)
        m = jnp.maximum(jnp.max(s, axis=-1, keepdims=True), sk)
        p = jnp.exp(s - m)
        denom = jnp.sum(p, axis=-1, keepdims=True) + jnp.exp(sk - m)
        o = jnp.dot(p.astype(BF16), v, preferred_element_type=F32) / denom
        heads += [o[hh * blk:(hh + 1) * blk] for hh in range(Q_PER_KV)]
    o_ref[0] = jnp.concatenate(heads, axis=1).astype(BF16)


def _attention(qh, kh, vh, sink, n_ctx):
    B, _, N, _ = qh.shape
    nq = N // ATTN_BLOCK
    kv = lambda f: pl.BlockSpec((1, ATTN_KV_HEADS, ATTN_BLOCK, HEAD_DIM), f)
    prev = lambda b, i: (b, 0, jnp.maximum(i - 1, 0), 0)
    cur = lambda b, i: (b, 0, i, 0)
    nxt = lambda b, i: (b, 0, jnp.minimum(i + 1, nq - 1), 0)
    ctx = pl.BlockSpec((1, ATTN_KV_HEADS, n_ctx, HEAD_DIM), lambda b, i: (b, 0, 0, 0))
    return pl.pallas_call(
        functools.partial(_attn_kernel, n_ctx=n_ctx, n_rows=N),
        out_shape=jax.ShapeDtypeStruct((B, N, ATTN_DIM), BF16),
        grid=(B, nq),
        in_specs=[pl.BlockSpec(memory_space=pltpu.SMEM),
                  pl.BlockSpec((1, ATTN_HEADS, ATTN_BLOCK, HEAD_DIM), cur),
                  kv(prev), kv(cur), kv(nxt), ctx, kv(prev), kv(cur), kv(nxt), ctx],
        out_specs=pl.BlockSpec((1, ATTN_BLOCK, ATTN_DIM), lambda b, i: (b, i, 0)),
        compiler_params=_cparams("arbitrary", "arbitrary"),
        name="band_attention",
    )(sink, qh, kh, kh, kh, kh, vh, vh, vh, vh)


def _hgrn_tables(reverse):
    c = HG_CHUNK
    t = np.arange(c)
    msum = np.zeros((HG_LEVELS + 1, c, c), np.float32)
    qm = np.zeros((HG_LEVELS, c, 1), np.float32)
    km = np.zeros((HG_LEVELS, c, 1), np.float32)
    sm = np.zeros((HG_LEVELS, c, c), np.float32)
    qm[0] = 1.0
    km[0] = 1.0
    sm[0] = np.eye(c)
    for j in range(1, HG_LEVELS):
        m = c >> (j - 1)
        half = m // 2
        later = (t % m) >= half
        mid = (t // m) * m + half
        r = t[None, :]
        rows_later = later[:, None] & (r > mid[:, None]) & (r <= t[:, None])
        rows_early = (~later)[:, None] & (r > t[:, None]) & (r <= mid[:, None])
        msum[j - 1] = (rows_later | rows_early).astype(np.float32)
        qm[j, :, 0] = later
        km[j, :, 0] = ~later
        sm[j] = ((t[:, None] // m) == (t[None, :] // m)).astype(np.float32)
    msum[HG_LEVELS - 1] = (t[None, :] <= t[:, None]).astype(np.float32)
    msum[HG_LEVELS] = (t[None, :] > t[:, None]).astype(np.float32)
    if reverse:
        msum = msum[:, ::-1, ::-1]
        qm = qm[:, ::-1]
        km = km[:, ::-1]
        sm = sm[:, ::-1, ::-1]
    ones = np.ones((1, 1, HG_WIDTH), np.float32)
    return (jnp.asarray(msum.reshape(-1, c), BF16),
            jnp.asarray(qm * ones), jnp.asarray(km * ones),
            jnp.asarray(np.tile(sm, (1, 1, HG_HEADS))))


def _head_mask():
    h = np.arange(HG_WIDTH) // HG_DIM
    return (h[:, None] == h[None, :]).astype(np.float32)


def _hgrn_kernel(*refs, reverse):
    if reverse:
        (hq_ref, hi_ref, hz_ref, lb_ref, ms_ref, qm_ref, km_ref, sm_ref, hm_ref,
         of_ref, hg_ref, hmean_ref, ng_ref, o_ref, st_ref) = refs
    else:
        (hq_ref, hi_ref, hz_ref, lb_ref, ms_ref, qm_ref, km_ref, sm_ref, hm_ref,
         o_ref, st_ref) = refs
    c = HG_CHUNK
    n_chunks = ROW_TILE // c

    @pl.when(pl.program_id(1) == 0)
    def _():
        st_ref[...] = jnp.zeros_like(st_ref)

    lb = lb_ref[...]
    hm = hm_ref[...]
    outs = [None] * n_chunks
    order = range(n_chunks - 1, -1, -1) if reverse else range(n_chunks)
    for ci in order:
        rows = slice(ci * c, (ci + 1) * c)
        hq = hq_ref[0, rows, :]
        q = hq * jax.nn.sigmoid(hq)
        v = hi_ref[0, rows, :]
        z = hz_ref[0, rows, :]
        g = jnp.log(lb + (1.0 - lb) * jax.nn.sigmoid(z))
        kk = (1.0 - lb) * jax.nn.sigmoid(-z)
        g_hi = g.astype(BF16)
        g_lo = (g - g_hi.astype(F32)).astype(BF16)
        dsum = jnp.dot(ms_ref[...], jnp.concatenate([g_hi, g_lo], axis=1), preferred_element_type=F32)
        e = jnp.exp(dsum[:, :HG_WIDTH] + dsum[:, HG_WIDTH:])
        scores = jnp.zeros((c, HG_HEADS * c), F32)
        for lv in range(HG_LEVELS):
            if lv == 0:
                ql, kl = q, kk
            else:
                el = e[(lv - 1) * c:lv * c]
                ql = q * el * qm_ref[lv]
                kl = kk * el * km_ref[lv]
            kbd = (jnp.concatenate([kl] * HG_HEADS, axis=0) * hm).astype(BF16)
            s_l = lax.dot_general(ql.astype(BF16), kbd, (((1,), (1,)), ((), ())),
                                  preferred_element_type=F32)
            scores = scores + s_l * sm_ref[lv]
        vbd = (jnp.concatenate([v] * HG_HEADS, axis=0) * hm).astype(BF16)
        o = jnp.dot(scores.astype(BF16), vbd, preferred_element_type=F32)
        e_cum = e[(HG_LEVELS - 1) * c:HG_LEVELS * c]
        st = st_ref[...]
        o = o + lax.dot_general((q * e_cum).astype(BF16), st.astype(BF16), (((1,), (1,)), ((), ())),
                                preferred_element_type=F32)
        k_dec = kk * e[HG_LEVELS * c:(HG_LEVELS + 1) * c]
        e_end = e_cum[0:1] if reverse else e_cum[c - 1:c]
        upd = jnp.dot(v.T.astype(BF16), k_dec.astype(BF16), preferred_element_type=F32)
        st_ref[...] = st * e_end + upd * hm
        outs[ci] = o
    o_all = jnp.concatenate(outs, axis=0)
    if reverse:
        o_sum = of_ref[0] + o_all
        ms = jnp.dot((o_sum * o_sum).astype(BF16), hmean_ref[...], preferred_element_type=F32)
        gate = hg_ref[0]
        o_ref[0] = (o_sum * lax.rsqrt(ms + NORM_EPS) * ng_ref[...]
                    * (gate * jax.nn.sigmoid(gate))).astype(BF16)
    else:
        o_ref[0] = o_all


def _hgrn_scan(hq, hi, hz, lb, reverse, o_fwd=None, hgate=None, norm_g=None):
    B, N, _ = hq.shape
    nt = N // ROW_TILE
    if reverse:
        order = lambda b, i: (b, jnp.where(i == 0, 0, nt - i), 0)
    else:
        order = lambda b, i: (b, i, 0)
    row = pl.BlockSpec((1, ROW_TILE, HG_WIDTH), order)
    const = lambda a: pl.BlockSpec(a.shape, lambda b, i: (0,) * a.ndim)
    msum, qm, km, sm = _hgrn_tables(reverse)
    hm = jnp.asarray(_head_mask())
    args = [hq, hi, hz, lb.reshape(1, HG_WIDTH), msum, qm, km, sm, hm]
    specs = [row, row, row] + [const(a) for a in args[3:]]
    if reverse:
        hmean = jnp.asarray(_head_mask() / HG_DIM, BF16)
        ng = jnp.tile(norm_g.reshape(1, HG_DIM), (1, HG_HEADS))
        args += [o_fwd, hgate, hmean, ng]
        specs += [row, row, const(hmean), const(ng)]
    return pl.pallas_call(
        functools.partial(_hgrn_kernel, reverse=reverse),
        out_shape=jax.ShapeDtypeStruct((B, N, HG_WIDTH), BF16 if reverse else F32),
        grid=(B, nt),
        in_specs=specs,
        out_specs=row,
        scratch_shapes=[pltpu.VMEM((HG_WIDTH, HG_WIDTH), F32)],
        compiler_params=_cparams("arbitrary", "arbitrary"),
        name="hgrn_bwd" if reverse else "hgrn_fwd",
    )(*args)


POOL_HALO = 8


def _pool_kernel(up_ref, u_ref, un_ref, w_ref, sc_ref, o_ref, e_ref, *, n_ctx, n_lat):
    i = pl.program_id(1)
    nt = pl.num_programs(1)
    u = u_ref[0]
    has_prev = i >= 2
    has_next = (i >= 1) & (i <= nt - 2)
    e_ref[0:POOL_HALO, :] = jnp.where(has_prev, up_ref[0], 0.0)
    e_ref[POOL_HALO:POOL_HALO + ROW_TILE, :] = u
    e_ref[POOL_HALO + ROW_TILE:, :] = jnp.where(has_next, un_ref[0], 0.0)
    seg_start = jnp.where(i == 0, 0, n_ctx)
    seg_len = jnp.where(i == 0, n_ctx, n_lat)
    tau = i * ROW_TILE - seg_start + lax.broadcasted_iota(jnp.int32, (ROW_TILE, 1), 0)
    lane_grp = lax.broadcasted_iota(jnp.int32, (1, 2 * POOL_GROUP_DIM), 1) // POOL_GROUP_DIM
    halves = []
    for hb in range(2):
        w_a, w_b = POOL_WINDOWS[2 * hb], POOL_WINDOWS[2 * hb + 1]
        lanes = slice(2 * POOL_GROUP_DIM * hb, 2 * POOL_GROUP_DIM * (hb + 1))
        acc = jnp.zeros((ROW_TILE, 2 * POOL_GROUP_DIM), F32)
        for off in range(-((w_b - 1) // 2), w_b // 2 + 1):
            x = e_ref[POOL_HALO + off:POOL_HALO + off + ROW_TILE, lanes]
            if -((w_a - 1) // 2) <= off <= w_a // 2:
                acc = acc + x
            else:
                acc = acc + jnp.where(lane_grp == 1, x, 0.0)

        def count(w):
            lo = jnp.maximum(tau - (w - 1) // 2, 0)
            hi = jnp.minimum(tau + w // 2 + 1, seg_len)
            return (hi - lo).astype(F32)

        cnt = jnp.where(lane_grp == 0, count(w_a), count(w_b))
        halves.append(acc / cnt)
    pooled = jnp.concatenate(halves, axis=1) - u
    y = jnp.dot(pooled.astype(BF16), w_ref[...], preferred_element_type=F32) * sc_ref[...]
    o_ref[0] = y.astype(BF16)


def _pool(u, w_bd, scale, n_ctx):
    B, N, _ = u.shape
    nt = N // ROW_TILE
    per = ROW_TILE // POOL_HALO
    nh = N // POOL_HALO
    return pl.pallas_call(
        functools.partial(_pool_kernel, n_ctx=n_ctx, n_lat=N - n_ctx),
        out_shape=jax.ShapeDtypeStruct((B, N, POOL_DIM), BF16),
        grid=(B, nt),
        in_specs=[
            pl.BlockSpec((1, POOL_HALO, POOL_DIM), lambda b, i: (b, jnp.maximum(i * per - 1, 0), 0)),
            pl.BlockSpec((1, ROW_TILE, POOL_DIM), lambda b, i: (b, i, 0)),
            pl.BlockSpec((1, POOL_HALO, POOL_DIM), lambda b, i: (b, jnp.minimum((i + 1) * per, nh - 1), 0)),
            pl.BlockSpec((POOL_DIM, POOL_DIM), lambda b, i: (0, 0)),
            pl.BlockSpec((1, POOL_DIM), lambda b, i: (0, 0)),
        ],
        out_specs=pl.BlockSpec((1, ROW_TILE, POOL_DIM), lambda b, i: (b, i, 0)),
        scratch_shapes=[pltpu.VMEM((ROW_TILE + 2 * POOL_HALO, POOL_DIM), F32)],
        compiler_params=_cparams("arbitrary", "arbitrary"),
        name="pool_mixer",
    )(u, u, u, w_bd, scale.reshape(1, POOL_DIM))


def _dot_split(a, b_hi, b_lo):
    a_hi = a.astype(BF16)
    a_lo = (a - a_hi.astype(F32)).astype(BF16)
    return (jnp.dot(a_hi, b_hi, preferred_element_type=F32)
            + jnp.dot(a_hi, b_lo, preferred_element_type=F32)
            + jnp.dot(a_lo, b_hi, preferred_element_type=F32))


def _out_proj_kernel(a_ref, b_ref, c_ref, x_ref, w_ref, mod_ref, g2_ref, rh_ref, rl_ref, rb_ref,
                     xo_ref, h2_ref, lg_ref):
    mix = (jnp.dot(a_ref[0], w_ref[0:POOL_DIM, :], preferred_element_type=F32)
           + jnp.dot(b_ref[0], w_ref[POOL_DIM:POOL_DIM + ATTN_DIM, :], preferred_element_type=F32)
           + jnp.dot(c_ref[0], w_ref[POOL_DIM + ATTN_DIM:, :], preferred_element_type=F32))
    mod = mod_ref[0]
    gate1 = mod[:, 2 * D_MODEL:3 * D_MODEL]
    sh2 = mod[:, 3 * D_MODEL:4 * D_MODEL]
    sc2 = mod[:, 4 * D_MODEL:5 * D_MODEL]
    x = x_ref[0] + gate1 * mix
    xo_ref[0] = x
    h2 = x * _rms_scale(x) * g2_ref[...] * (1.0 + sc2) + sh2
    h2_ref[0] = h2.astype(BF16)
    lg_ref[0] = _dot_split(h2, rh_ref[...], rl_ref[...]) + rb_ref[...]


def _out_proj(a, b, c, xs, w_out_bf16, mod, g2, r_hi, r_lo, r_b):
    B, N, _ = xs.shape
    nt = N // ROW_TILE
    row = lambda w: pl.BlockSpec((1, ROW_TILE, w), lambda b_, i: (b_, i, 0))
    full = lambda s: pl.BlockSpec(s, lambda b_, i: (0,) * len(s))
    return pl.pallas_call(
        _out_proj_kernel,
        out_shape=[jax.ShapeDtypeStruct((B, N, D_MODEL), F32),
                   jax.ShapeDtypeStruct((B, N, D_MODEL), BF16),
                   jax.ShapeDtypeStruct((B, N, ROUTER_COLS), F32)],
        grid=(B, nt),
        in_specs=[row(POOL_DIM), row(ATTN_DIM), row(HG_WIDTH), row(D_MODEL),
                  full((MIX_WIDTH, D_MODEL)), _mod_spec(B), full((1, D_MODEL)),
                  full((D_MODEL, ROUTER_COLS)), full((D_MODEL, ROUTER_COLS)), full((1, ROUTER_COLS))],
        out_specs=[row(D_MODEL), row(D_MODEL), row(ROUTER_COLS)],
        compiler_params=_cparams("arbitrary", "arbitrary"),
        name="out_proj_router",
    )(a, b, c, xs, w_out_bf16, mod, g2.reshape(1, D_MODEL), r_hi, r_lo, r_b)


def _ffn_kernel(be_ref, x_ref, wg_ref, wu_ref, wd_ref, o_ref):
    del be_ref
    x = x_ref[...]
    gate = jnp.dot(x, wg_ref[0].astype(BF16), preferred_element_type=F32)
    up = jnp.dot(x, wu_ref[0].astype(BF16), preferred_element_type=F32)
    hid = (gate * jax.nn.sigmoid(gate) * up).astype(BF16)
    o_ref[...] = jnp.dot(hid, wd_ref[0].astype(BF16), preferred_element_type=F32)


def _expert_ffn(block_expert, xs_sorted, w_gate, w_up, w_down):
    P = xs_sorted.shape[0]
    n_blocks = P // MOE_BLOCK
    grid_spec = pltpu.PrefetchScalarGridSpec(
        num_scalar_prefetch=1,
        grid=(n_blocks,),
        in_specs=[
            pl.BlockSpec((MOE_BLOCK, D_MODEL), lambda i, be: (i, 0)),
            pl.BlockSpec((1, D_MODEL, EXPERT_HIDDEN), lambda i, be: (be[i], 0, 0)),
            pl.BlockSpec((1, D_MODEL, EXPERT_HIDDEN), lambda i, be: (be[i], 0, 0)),
            pl.BlockSpec((1, EXPERT_HIDDEN, D_MODEL), lambda i, be: (be[i], 0, 0)),
        ],
        out_specs=pl.BlockSpec((MOE_BLOCK, D_MODEL), lambda i, be: (i, 0)),
    )
    return pl.pallas_call(
        _ffn_kernel,
        out_shape=jax.ShapeDtypeStruct((P, D_MODEL), F32),
        grid_spec=grid_spec,
        compiler_params=_cparams("arbitrary"),
        name="expert_ffn",
    )(block_expert, xs_sorted, w_gate, w_up, w_down)


def _route(logits):
    T = logits.shape[0]
    lg = logits[:, :N_GROUPS]
    pg = jax.nn.softmax(lg, axis=-1)
    g_idx = jnp.argmax(lg, axis=-1)[:, None]
    p_grp = jnp.take_along_axis(pg, g_idx, axis=-1)
    le = logits[:, N_GROUPS:N_GROUPS + N_EXPERTS].reshape(T, N_GROUPS, EXPERTS_PER_GROUP)
    le = jnp.take_along_axis(le, g_idx[:, :, None], axis=1)[:, 0]
    pe = jax.nn.softmax(le, axis=-1)
    w_top, e_top = lax.top_k(pe, TOP_K)
    w_top = w_top / jnp.sum(w_top, axis=-1, keepdims=True)
    return g_idx * EXPERTS_PER_GROUP + e_top, p_grp * w_top


def _moe(h2, logits, w_gate, w_up, w_down):
    T = h2.shape[0]
    expert_idx, gate_w = _route(logits)
    A = T * TOP_K
    flat_e = expert_idx.reshape(-1).astype(jnp.int32)
    order = jnp.argsort(flat_e)
    sorted_e = flat_e[order]
    counts = jnp.bincount(flat_e, length=N_EXPERTS)
    padded = ((counts + MOE_BLOCK - 1) // MOE_BLOCK) * MOE_BLOCK
    padded_end = jnp.cumsum(padded)
    padded_start = padded_end - padded
    start = jnp.cumsum(counts) - counts
    dest = padded_start[sorted_e] + (jnp.arange(A) - start[sorted_e])
    n_blocks = -(-A // MOE_BLOCK) + N_EXPERTS
    P = n_blocks * MOE_BLOCK
    token_sorted = (order // TOP_K).astype(jnp.int32)
    token_of_slot = jnp.zeros((P,), jnp.int32).at[dest].set(token_sorted)
    block_expert = jnp.minimum(
        jnp.searchsorted(padded_end, jnp.arange(n_blocks) * MOE_BLOCK, side='right'),
        N_EXPERTS - 1).astype(jnp.int32)
    ys = _expert_ffn(block_expert, h2[token_of_slot], w_gate, w_up, w_down)
    contrib = ys[dest] * gate_w.reshape(-1)[order][:, None]
    return jax.ops.segment_sum(contrib, token_sorted, num_segments=T)


def _rope_tables(n_ctx, n_lat):
    row = jnp.repeat(jnp.arange(n_lat // GRID_W, dtype=F32), GRID_W)
    col = jnp.tile(jnp.arange(GRID_W, dtype=F32), n_lat // GRID_W)
    inv = ROPE_BASE ** (-jnp.arange(0, ROPE_AXIS_DIM, 2, dtype=F32) / ROPE_AXIS_DIM)
    ang = jnp.concatenate([row[:, None] * inv, col[:, None] * inv], axis=-1)
    cos = jnp.concatenate([jnp.ones((n_ctx, ROPE_AXIS_DIM), F32), jnp.cos(ang)], axis=0)
    sin = jnp.concatenate([jnp.zeros((n_ctx, ROPE_AXIS_DIM), F32), jnp.sin(ang)], axis=0)
    return jnp.tile(cos, (1, 4)), jnp.tile(jnp.concatenate([-sin, sin], axis=1), (1, 2))


def _block_diag(w):
    g, n, _ = w.shape
    out = jnp.zeros((g * n, g * n), w.dtype)
    for i in range(g):
        out = out.at[i * n:(i + 1) * n, i * n:(i + 1) * n].set(w[i])
    return out


def kernel(x, c, ctx, c_ctx, norm1_g, norm2_g, ada_w, ada_b, w_in, w_out, pool_w, pool_scale,
           q_norm_g, k_norm_g, attn_sink, hgrn_lb, hgrn_norm_g, router_group_w, router_group_b,
           router_expert_w, router_expert_b, expert_w_gate, expert_w_up, expert_w_down):
    B, L, _ = x.shape
    C = ctx.shape[1]
    N = C + L
    depth = ada_w.shape[0]
    assert C == ROW_TILE and L % ROW_TILE == 0 and L % GRID_W == 0

    cos2, sin2 = _rope_tables(C, L)
    sm = jax.nn.softmax(hgrn_lb.astype(F32), axis=0)
    lb_all = jnp.cumsum(sm, axis=0) - sm[0]

    craw = jnp.concatenate([c, c_ctx[None], jnp.zeros((8 - B - 1, D_MODEL), F32)], axis=0)
    mod_all = _ada(craw, ada_w, ada_b)[:, :B + 1].reshape(depth, B + 1, 1, 6 * D_MODEL)

    xs = jnp.concatenate([ctx, x], axis=1)
    for l in range(depth):
        mod = mod_all[l]
        u, q, k, v, hq, hi, hzf, hzb, hgate = _in_proj(xs, norm1_g[l], mod, w_in[l].astype(BF16))
        a = _pool(u, _block_diag(pool_w[l]).astype(BF16), pool_scale[l], C)
        qh, kh, vh = _qk_prep(q, k, v, cos2, sin2,
                              jnp.tile(q_norm_g[l].reshape(1, HEAD_DIM), (1, 2)),
                              jnp.tile(k_norm_g[l].reshape(1, HEAD_DIM), (1, 2)))
        b_mix = _attention(qh, kh, vh, attn_sink[l], C)
        o_f = _hgrn_scan(hq, hi, hzf, lb_all[l, 0], reverse=False)
        c_mix = _hgrn_scan(hq, hi, hzb, lb_all[l, 1], reverse=True, o_fwd=o_f, hgate=hgate,
                           norm_g=hgrn_norm_g[l])
        r_w = jnp.concatenate([router_group_w[l], router_expert_w[l],
                               jnp.zeros((D_MODEL, ROUTER_COLS - N_GROUPS - N_EXPERTS), F32)], axis=1)
        r_b = jnp.concatenate([router_group_b[l], router_expert_b[l],
                               jnp.zeros((ROUTER_COLS - N_GROUPS - N_EXPERTS,), F32)]).reshape(1, ROUTER_COLS)
        r_hi = r_w.astype(BF16)
        r_lo = (r_w - r_hi.astype(F32)).astype(BF16)
        xs, h2, logits = _out_proj(a, b_mix, c_mix, xs, w_out[l].astype(BF16), mod, norm2_g[l],
                                   r_hi, r_lo, r_b)
        y = _moe(h2.reshape(B * N, D_MODEL), logits.reshape(B * N, ROUTER_COLS),
                 expert_w_gate[l], expert_w_up[l], expert_w_down[l])
        gate2 = jnp.concatenate([jnp.broadcast_to(mod[B:B + 1, :, 5 * D_MODEL:], (B, 1, D_MODEL)),
                                 mod[:B, :, 5 * D_MODEL:]], axis=1)
        gate_rows = jnp.where((jnp.arange(N) < C)[None, :, None], gate2[:, 0:1], gate2[:, 1:2])
        xs = xs + gate_rows * y.reshape(B, N, D_MODEL)
    return xs[:, C:]
```

```python
import functools

import numpy as np
import jax
import jax.numpy as jnp
from jax import lax
from jax.experimental import pallas as pl
from jax.experimental.pallas import tpu as pltpu

D_MODEL = 1024
DEPTH = 4
GRID_W = 64
NORM_EPS = 1e-6
NEG_INF = -1e30

POOL_WINDOWS = (2, 4, 8, 16)
POOL_GROUP_DIM = 64
POOL_DIM = 256

HEAD_DIM = 64
ATTN_HEADS = 8
ATTN_KV_HEADS = 2
Q_PER_KV = 4
ATTN_DIM = 512
KV_DIM = 128
ATTN_WINDOW = 128
ATTN_BLOCK = 128
ROPE_BASE = 10000.0
ROPE_AXIS_DIM = 32

HG_HEADS = 4
HG_DIM = 64
HG_WIDTH = 256
HG_CHUNK = 64
HG_LEVELS = 7

MIX_WIDTH = 1024
IN_SIZES = (POOL_DIM, ATTN_DIM, KV_DIM, KV_DIM, HG_WIDTH, HG_WIDTH, HG_WIDTH, HG_WIDTH, HG_WIDTH)
IN_OFFS = tuple(int(sum(IN_SIZES[:i])) for i in range(len(IN_SIZES) + 1))
IN_COLS = IN_OFFS[-1]

N_GROUPS = 4
EXPERTS_PER_GROUP = 8
N_EXPERTS = 32
TOP_K = 2
EXPERT_HIDDEN = 512
MOE_BLOCK = 256
ROUTER_ROWS = 128
ROUTER_EXPERT_ROW0 = 8
MOE_ALIGN = 8
MOE_CAP = 32

ROW_TILE = 256
MOE_TILE_SLOTS = TOP_K * ROW_TILE + N_EXPERTS * MOE_ALIGN
ADA_COL_TILE = 1536
VMEM_LIMIT = 56 * 1024 * 1024

F32 = jnp.float32
BF16 = jnp.bfloat16


def _cparams(*sem):
    return pltpu.CompilerParams(dimension_semantics=sem, vmem_limit_bytes=VMEM_LIMIT)


def _rms_scale(x):
    return lax.rsqrt(jnp.mean(x * x, axis=-1, keepdims=True) + NORM_EPS)


def _ada_kernel(c_ref, w_ref, b_ref, o_ref):
    s = c_ref[...]
    s = (s * jax.nn.sigmoid(s)).astype(BF16)
    o_ref[0] = jnp.dot(s, w_ref[0].astype(BF16), preferred_element_type=F32) + b_ref[0]


def _ada(craw, ada_w, ada_b):
    depth = ada_w.shape[0]
    ncol = ada_w.shape[2]
    return pl.pallas_call(
        _ada_kernel,
        out_shape=jax.ShapeDtypeStruct((depth, 8, ncol), F32),
        grid=(depth, ncol // ADA_COL_TILE),
        in_specs=[
            pl.BlockSpec((8, D_MODEL), lambda l, j: (0, 0)),
            pl.BlockSpec((1, D_MODEL, ADA_COL_TILE), lambda l, j: (l, 0, j)),
            pl.BlockSpec((1, 1, ADA_COL_TILE), lambda l, j: (l, 0, j)),
        ],
        out_specs=pl.BlockSpec((1, 8, ADA_COL_TILE), lambda l, j: (l, 0, j)),
        compiler_params=_cparams("arbitrary", "arbitrary"),
        name="ada_mod",
    )(craw, ada_w, ada_b.reshape(depth, 1, ncol))


def _in_proj_kernel(x_ref, g_ref, mod_ref, w_ref, *out_refs):
    x = x_ref[0]
    mod = mod_ref[0]
    sh = mod[:, 0:D_MODEL]
    sc = mod[:, D_MODEL:2 * D_MODEL]
    h = (x * _rms_scale(x) * g_ref[...] * (1.0 + sc) + sh).astype(BF16)
    for o_ref, lo, hi in zip(out_refs, IN_OFFS[:-1], IN_OFFS[1:]):
        o_ref[0] = jnp.dot(h, w_ref[:, lo:hi], preferred_element_type=F32)


def _mod_spec(n_batch):
    return pl.BlockSpec((1, 1, 6 * D_MODEL), lambda b, i: (jnp.where(i == 0, n_batch, b), 0, 0))


def _in_proj(xs, g1, mod, w_in_bf16):
    B, N, _ = xs.shape
    nt = N // ROW_TILE
    return pl.pallas_call(
        _in_proj_kernel,
        out_shape=[jax.ShapeDtypeStruct((B, N, w), F32) for w in IN_SIZES],
        grid=(B, nt),
        in_specs=[
            pl.BlockSpec((1, ROW_TILE, D_MODEL), lambda b, i: (b, i, 0)),
            pl.BlockSpec((1, D_MODEL), lambda b, i: (0, 0)),
            _mod_spec(B),
            pl.BlockSpec((D_MODEL, IN_COLS), lambda b, i: (0, 0)),
        ],
        out_specs=[pl.BlockSpec((1, ROW_TILE, w), lambda b, i: (b, i, 0)) for w in IN_SIZES],
        compiler_params=_cparams("arbitrary", "arbitrary"),
        name="in_proj",
    )(xs, g1.reshape(1, D_MODEL), mod, w_in_bf16)


def _qk_prep_kernel(q_ref, k_ref, v_ref, cos_ref, sin_ref, gq_ref, gk_ref, qh_ref, kh_ref, vh_ref):
    cos = cos_ref[...]
    sin = sin_ref[...]
    lane = lax.broadcasted_iota(jnp.int32, (1, 2 * HEAD_DIM), 1)
    lo_head = lane < HEAD_DIM
    first_half = (lane % HEAD_DIM) < (HEAD_DIM // 2)

    def norm_rope(x2, g, scale):
        sq = x2 * x2
        s0 = jnp.sum(jnp.where(lo_head, sq, 0.0), axis=-1, keepdims=True)
        s1 = jnp.sum(jnp.where(lo_head, 0.0, sq), axis=-1, keepdims=True)
        ms = jnp.where(lo_head, s0, s1) * (1.0 / HEAD_DIM)
        xn = x2 * lax.rsqrt(ms + NORM_EPS) * g
        swapped = jnp.where(first_half, pltpu.roll(xn, 2 * HEAD_DIM - HEAD_DIM // 2, 1),
                            pltpu.roll(xn, HEAD_DIM // 2, 1))
        return (xn * cos + swapped * sin) * scale

    for p in range(ATTN_HEADS // 2):
        y = norm_rope(q_ref[0, :, 2 * HEAD_DIM * p:2 * HEAD_DIM * (p + 1)], gq_ref[...], HEAD_DIM ** -0.5)
        qh_ref[0, 2 * p] = y[:, :HEAD_DIM].astype(BF16)
        qh_ref[0, 2 * p + 1] = y[:, HEAD_DIM:].astype(BF16)
    y = norm_rope(k_ref[0], gk_ref[...], 1.0)
    kh_ref[0, 0] = y[:, :HEAD_DIM].astype(BF16)
    kh_ref[0, 1] = y[:, HEAD_DIM:].astype(BF16)
    v = v_ref[0]
    vh_ref[0, 0] = v[:, :HEAD_DIM].astype(BF16)
    vh_ref[0, 1] = v[:, HEAD_DIM:].astype(BF16)


def _qk_prep(q, k, v, cos2, sin2, gq2, gk2):
    B, N, _ = q.shape
    nt = N // ROW_TILE
    row = lambda w: pl.BlockSpec((1, ROW_TILE, w), lambda b, i: (b, i, 0))
    tab = pl.BlockSpec((ROW_TILE, 2 * HEAD_DIM), lambda b, i: (i, 0))
    vec = pl.BlockSpec((1, 2 * HEAD_DIM), lambda b, i: (0, 0))
    head = lambda n: pl.BlockSpec((1, n, ROW_TILE, HEAD_DIM), lambda b, i: (b, 0, i, 0))
    return pl.pallas_call(
        _qk_prep_kernel,
        out_shape=[jax.ShapeDtypeStruct((B, ATTN_HEADS, N, HEAD_DIM), BF16),
                   jax.ShapeDtypeStruct((B, ATTN_KV_HEADS, N, HEAD_DIM), BF16),
                   jax.ShapeDtypeStruct((B, ATTN_KV_HEADS, N, HEAD_DIM), BF16)],
        grid=(B, nt),
        in_specs=[row(ATTN_DIM), row(KV_DIM), row(KV_DIM), tab, tab, vec, vec],
        out_specs=[head(ATTN_HEADS), head(ATTN_KV_HEADS), head(ATTN_KV_HEADS)],
        compiler_params=_cparams("arbitrary", "arbitrary"),
        name="qk_prep",
    )(q, k, v, cos2, sin2, gq2, gk2)


def _attn_kernel(sink_ref, q_ref, kp_ref, kc_ref, kn_ref, kx_ref, vp_ref, vc_ref, vn_ref, vx_ref,
                 o_ref, *, n_ctx, n_rows):
    i = pl.program_id(1)
    blk = ATTN_BLOCK
    t = i * blk + lax.broadcasted_iota(jnp.int32, (blk, 1), 0)
    kr = (i - 1) * blk + lax.broadcasted_iota(jnp.int32, (1, 3 * blk), 1)
    ok_loc = (t >= n_ctx) & (kr >= n_ctx) & (kr < n_rows) & (jnp.abs(t - kr) <= ATTN_WINDOW)
    bias = jnp.concatenate([jnp.where(ok_loc, 0.0, NEG_INF), jnp.zeros((blk, n_ctx), F32)],
                           axis=1)[None]
    heads = []
    for g in range(ATTN_KV_HEADS):
        q = q_ref[0, Q_PER_KV * g:Q_PER_KV * (g + 1)].reshape(Q_PER_KV * blk, HEAD_DIM)
        k = jnp.concatenate([kp_ref[0, g], kc_ref[0, g], kn_ref[0, g], kx_ref[0, g]], axis=0)
        v = jnp.concatenate([vp_ref[0, g], vc_ref[0, g], vn_ref[0, g], vx_ref[0, g]], axis=0)
        s = lax.dot_general(q, k, (((1,), (1,)), ((), ())), preferred_element_type=F32)
        nk = s.shape[-1]
        s = (s.reshape(Q_PER_KV, blk, nk) + bias).reshape(Q_PER_KV * blk, nk)
        sk = jnp.concatenate(
            [jnp.full((blk, 1), sink_ref[Q_PER_KV * g + hh], F32) for hh in range(Q_PER_KV)], axis=0)
        m = jnp.maximum(jnp.max(s, axis=-1, keepdims=True), sk)
        p = jnp.exp(s - m)
        denom = jnp.sum(p, axis=-1, keepdims=True) + jnp.exp(sk - m)
        o = jnp.dot(p.astype(BF16), v, preferred_element_type=F32) / denom
        heads += [o[hh * blk:(hh + 1) * blk] for hh in range(Q_PER_KV)]
    o_ref[0] = jnp.concatenate(heads, axis=1).astype(BF16)


def _attention(qh, kh, vh, sink, n_ctx):
    B, _, N, _ = qh.shape
    nq = N // ATTN_BLOCK
    kv = lambda f: pl.BlockSpec((1, ATTN_KV_HEADS, ATTN_BLOCK, HEAD_DIM), f)
    prev = lambda b, i: (b, 0, jnp.maximum(i - 1, 0), 0)
    cur = lambda b, i: (b, 0, i, 0)
    nxt = lambda b, i: (b, 0, jnp.minimum(i + 1, nq - 1), 0)
    ctx = pl.BlockSpec((1, ATTN_KV_HEADS, n_ctx, HEAD_DIM), lambda b, i: (b, 0, 0, 0))
    return pl.pallas_call(
        functools.partial(_attn_kernel, n_ctx=n_ctx, n_rows=N),
        out_shape=jax.ShapeDtypeStruct((B, N, ATTN_DIM), BF16),
        grid=(B, nq),
        in_specs=[pl.BlockSpec(memory_space=pltpu.SMEM),
                  pl.BlockSpec((1, ATTN_HEADS, ATTN_BLOCK, HEAD_DIM), cur),
                  kv(prev), kv(cur), kv(nxt), ctx, kv(prev), kv(cur), kv(nxt), ctx],
        out_specs=pl.BlockSpec((1, ATTN_BLOCK, ATTN_DIM), lambda b, i: (b, i, 0)),
        compiler_params=_cparams("arbitrary", "arbitrary"),
        name="band_attention",
    )(sink, qh, kh, kh, kh, kh, vh, vh, vh, vh)


def _hgrn_tables(reverse):
    c = HG_CHUNK
    t = np.arange(c)
    msum = np.zeros((HG_LEVELS + 1, c, c), np.float32)
    qm = np.zeros((HG_LEVELS, c, 1), np.float32)
    km = np.zeros((HG_LEVELS, c, 1), np.float32)
    sm = np.zeros((HG_LEVELS, c, c), np.float32)
    qm[0] = 1.0
    km[0] = 1.0
    sm[0] = np.eye(c)
    for j in range(1, HG_LEVELS):
        m = c >> (j - 1)
        half = m // 2
        later = (t % m) >= half
        mid = (t // m) * m + half
        r = t[None, :]
        rows_later = later[:, None] & (r > mid[:, None]) & (r <= t[:, None])
        rows_early = (~later)[:, None] & (r > t[:, None]) & (r <= mid[:, None])
        msum[j - 1] = (rows_later | rows_early).astype(np.float32)
        qm[j, :, 0] = later
        km[j, :, 0] = ~later
        sm[j] = ((t[:, None] // m) == (t[None, :] // m)).astype(np.float32)
    msum[HG_LEVELS - 1] = (t[None, :] <= t[:, None]).astype(np.float32)
    msum[HG_LEVELS] = (t[None, :] > t[:, None]).astype(np.float32)
    if reverse:
        msum = msum[:, ::-1, ::-1]
        qm = qm[:, ::-1]
        km = km[:, ::-1]
        sm = sm[:, ::-1, ::-1]
    ones = np.ones((1, 1, HG_WIDTH), np.float32)
    return (jnp.asarray(msum.reshape(-1, c), BF16),
            jnp.asarray(qm * ones), jnp.asarray(km * ones),
            jnp.asarray(np.tile(sm, (1, 1, HG_HEADS))))


def _head_mask():
    h = np.arange(HG_WIDTH) // HG_DIM
    return (h[:, None] == h[None, :]).astype(np.float32)


def _hgrn_kernel(*refs, reverse):
    if reverse:
        (hq_ref, hi_ref, hz_ref, lb_ref, ms_ref, qm_ref, km_ref, sm_ref, hm_ref,
         of_ref, hg_ref, hmean_ref, ng_ref, o_ref, st_ref) = refs
    else:
        (hq_ref, hi_ref, hz_ref, lb_ref, ms_ref, qm_ref, km_ref, sm_ref, hm_ref,
         o_ref, st_ref) = refs
    c = HG_CHUNK
    n_chunks = ROW_TILE // c

    @pl.when(pl.program_id(1) == 0)
    def _():
        st_ref[...] = jnp.zeros_like(st_ref)

    lb = lb_ref[...]
    hm = hm_ref[...]
    outs = [None] * n_chunks
    order = range(n_chunks - 1, -1, -1) if reverse else range(n_chunks)
    for ci in order:
        rows = slice(ci * c, (ci + 1) * c)
        hq = hq_ref[0, rows, :]
        q = hq * jax.nn.sigmoid(hq)
        v = hi_ref[0, rows, :]
        z = hz_ref[0, rows, :]
        g = jnp.log(lb + (1.0 - lb) * jax.nn.sigmoid(z))
        kk = (1.0 - lb) * jax.nn.sigmoid(-z)
        g_hi = g.astype(BF16)
        g_lo = (g - g_hi.astype(F32)).astype(BF16)
        dsum = jnp.dot(ms_ref[...], jnp.concatenate([g_hi, g_lo], axis=1), preferred_element_type=F32)
        e = jnp.exp(dsum[:, :HG_WIDTH] + dsum[:, HG_WIDTH:])
        scores = jnp.zeros((c, HG_HEADS * c), F32)
        for lv in range(HG_LEVELS):
            if lv == 0:
                ql, kl = q, kk
            else:
                el = e[(lv - 1) * c:lv * c]
                ql = q * el * qm_ref[lv]
                kl = kk * el * km_ref[lv]
            kbd = (jnp.concatenate([kl] * HG_HEADS, axis=0) * hm).astype(BF16)
            s_l = lax.dot_general(ql.astype(BF16), kbd, (((1,), (1,)), ((), ())),
                                  preferred_element_type=F32)
            scores = scores + s_l * sm_ref[lv]
        vbd = (jnp.concatenate([v] * HG_HEADS, axis=0) * hm).astype(BF16)
        o = jnp.dot(scores.astype(BF16), vbd, preferred_element_type=F32)
        e_cum = e[(HG_LEVELS - 1) * c:HG_LEVELS * c]
        st = st_ref[...]
        o = o + lax.dot_general((q * e_cum).astype(BF16), st.astype(BF16), (((1,), (1,)), ((), ())),
                                preferred_element_type=F32)
        k_dec = kk * e[HG_LEVELS * c:(HG_LEVELS + 1) * c]
        e_end = e_cum[0:1] if reverse else e_cum[c - 1:c]
        upd = jnp.dot(v.T.astype(BF16), k_dec.astype(BF16), preferred_element_type=F32)
        st_ref[...] = st * e_end + upd * hm
        outs[ci] = o
    o_all = jnp.concatenate(outs, axis=0)
    if reverse:
        o_sum = of_ref[0] + o_all
        ms = jnp.dot((o_sum * o_sum).astype(BF16), hmean_ref[...], preferred_element_type=F32)
        gate = hg_ref[0]
        o_ref[0] = (o_sum * lax.rsqrt(ms + NORM_EPS) * ng_ref[...]
                    * (gate * jax.nn.sigmoid(gate))).astype(BF16)
    else:
        o_ref[0] = o_all


def _hgrn_scan(hq, hi, hz, lb, reverse, o_fwd=None, hgate=None, norm_g=None):
    B, N, _ = hq.shape
    nt = N // ROW_TILE
    if reverse:
        order = lambda b, i: (b, jnp.where(i == 0, 0, nt - i), 0)
    else:
        order = lambda b, i: (b, i, 0)
    row = pl.BlockSpec((1, ROW_TILE, HG_WIDTH), order)
    const = lambda a: pl.BlockSpec(a.shape, lambda b, i: (0,) * a.ndim)
    msum, qm, km, sm = _hgrn_tables(reverse)
    hm = jnp.asarray(_head_mask())
    args = [hq, hi, hz, lb.reshape(1, HG_WIDTH), msum, qm, km, sm, hm]
    specs = [row, row, row] + [const(a) for a in args[3:]]
    if reverse:
        hmean = jnp.asarray(_head_mask() / HG_DIM, BF16)
        ng = jnp.tile(norm_g.reshape(1, HG_DIM), (1, HG_HEADS))
        args += [o_fwd, hgate, hmean, ng]
        specs += [row, row, const(hmean), const(ng)]
    return pl.pallas_call(
        functools.partial(_hgrn_kernel, reverse=reverse),
        out_shape=jax.ShapeDtypeStruct((B, N, HG_WIDTH), BF16 if reverse else F32),
        grid=(B, nt),
        in_specs=specs,
        out_specs=row,
        scratch_shapes=[pltpu.VMEM((HG_WIDTH, HG_WIDTH), F32)],
        compiler_params=_cparams("arbitrary", "arbitrary"),
        name="hgrn_bwd" if reverse else "hgrn_fwd",
    )(*args)


POOL_HALO = 8


def _pool_kernel(up_ref, u_ref, un_ref, w_ref, sc_ref, o_ref, e_ref, *, n_ctx, n_lat):
    i = pl.program_id(1)
    nt = pl.num_programs(1)
    u = u_ref[0]
    has_prev = i >= 2
    has_next = (i >= 1) & (i <= nt - 2)
    e_ref[0:POOL_HALO, :] = jnp.where(has_prev, up_ref[0], 0.0)
    e_ref[POOL_HALO:POOL_HALO + ROW_TILE, :] = u
    e_ref[POOL_HALO + ROW_TILE:, :] = jnp.where(has_next, un_ref[0], 0.0)
    seg_start = jnp.where(i == 0, 0, n_ctx)
    seg_len = jnp.where(i == 0, n_ctx, n_lat)
    tau = i * ROW_TILE - seg_start + lax.broadcasted_iota(jnp.int32, (ROW_TILE, 1), 0)
    lane_grp = lax.broadcasted_iota(jnp.int32, (1, 2 * POOL_GROUP_DIM), 1) // POOL_GROUP_DIM
    halves = []
    for hb in range(2):
        w_a, w_b = POOL_WINDOWS[2 * hb], POOL_WINDOWS[2 * hb + 1]
        lanes = slice(2 * POOL_GROUP_DIM * hb, 2 * POOL_GROUP_DIM * (hb + 1))
        acc = jnp.zeros((ROW_TILE, 2 * POOL_GROUP_DIM), F32)
        for off in range(-((w_b - 1) // 2), w_b // 2 + 1):
            x = e_ref[POOL_HALO + off:POOL_HALO + off + ROW_TILE, lanes]
            if -((w_a - 1) // 2) <= off <= w_a // 2:
                acc = acc + x
            else:
                acc = acc + jnp.where(lane_grp == 1, x, 0.0)

        def count(w):
            lo = jnp.maximum(tau - (w - 1) // 2, 0)
            hi = jnp.minimum(tau + w // 2 + 1, seg_len)
            return (hi - lo).astype(F32)

        cnt = jnp.where(lane_grp == 0, count(w_a), count(w_b))
        halves.append(acc / cnt)
    pooled = jnp.concatenate(halves, axis=1) - u
    y = jnp.dot(pooled.astype(BF16), w_ref[...], preferred_element_type=F32) * sc_ref[...]
    o_ref[0] = y.astype(BF16)


def _pool(u, w_bd, scale, n_ctx):
    B, N, _ = u.shape
    nt = N // ROW_TILE
    per = ROW_TILE // POOL_HALO
    nh = N // POOL_HALO
    return pl.pallas_call(
        functools.partial(_pool_kernel, n_ctx=n_ctx, n_lat=N - n_ctx),
        out_shape=jax.ShapeDtypeStruct((B, N, POOL_DIM), BF16),
        grid=(B, nt),
        in_specs=[
            pl.BlockSpec((1, POOL_HALO, POOL_DIM), lambda b, i: (b, jnp.maximum(i * per - 1, 0), 0)),
            pl.BlockSpec((1, ROW_TILE, POOL_DIM), lambda b, i: (b, i, 0)),
            pl.BlockSpec((1, POOL_HALO, POOL_DIM), lambda b, i: (b, jnp.minimum((i + 1) * per, nh - 1), 0)),
            pl.BlockSpec((POOL_DIM, POOL_DIM), lambda b, i: (0, 0)),
            pl.BlockSpec((1, POOL_DIM), lambda b, i: (0, 0)),
        ],
        out_specs=pl.BlockSpec((1, ROW_TILE, POOL_DIM), lambda b, i: (b, i, 0)),
        scratch_shapes=[pltpu.VMEM((ROW_TILE + 2 * POOL_HALO, POOL_DIM), F32)],
        compiler_params=_cparams("arbitrary", "arbitrary"),
        name="pool_mixer",
    )(u, u, u, w_bd, scale.reshape(1, POOL_DIM))


def _out_proj_kernel(a_ref, b_ref, c_ref, x_ref, w_ref, mod_ref, g2_ref, rh_ref, rl_ref, rb_ref,
                     xo_ref, h2_ref, lg_ref):
    mix = (jnp.dot(a_ref[0], w_ref[0:POOL_DIM, :], preferred_element_type=F32)
           + jnp.dot(b_ref[0], w_ref[POOL_DIM:POOL_DIM + ATTN_DIM, :], preferred_element_type=F32)
           + jnp.dot(c_ref[0], w_ref[POOL_DIM + ATTN_DIM:, :], preferred_element_type=F32))
    mod = mod_ref[0]
    gate1 = mod[:, 2 * D_MODEL:3 * D_MODEL]
    sh2 = mod[:, 3 * D_MODEL:4 * D_MODEL]
    sc2 = mod[:, 4 * D_MODEL:5 * D_MODEL]
    x = x_ref[0] + gate1 * mix
    xo_ref[0] = x
    h2 = x * _rms_scale(x) * g2_ref[...] * (1.0 + sc2) + sh2
    h2_ref[0] = h2
    h_hi = h2.astype(BF16)
    h_lo = (h2 - h_hi.astype(F32)).astype(BF16)
    nt_dot = lambda r, h: lax.dot_general(r, h, (((1,), (1,)), ((), ())), preferred_element_type=F32)
    lg_ref[0] = (nt_dot(rh_ref[...], h_hi) + nt_dot(rh_ref[...], h_lo) + nt_dot(rl_ref[...], h_hi)
                 + rb_ref[...])


def _out_proj(a, b, c, xs, w_out_bf16, mod, g2, r_hi, r_lo, r_b):
    B, N, _ = xs.shape
    nt = N // ROW_TILE
    row = lambda w: pl.BlockSpec((1, ROW_TILE, w), lambda b_, i: (b_, i, 0))
    full = lambda s: pl.BlockSpec(s, lambda b_, i: (0,) * len(s))
    return pl.pallas_call(
        _out_proj_kernel,
        out_shape=[jax.ShapeDtypeStruct((B, N, D_MODEL), F32),
                   jax.ShapeDtypeStruct((B, N, D_MODEL), F32),
                   jax.ShapeDtypeStruct((B * nt, ROUTER_ROWS, ROW_TILE), F32)],
        grid=(B, nt),
        in_specs=[row(POOL_DIM), row(ATTN_DIM), row(HG_WIDTH), row(D_MODEL),
                  full((MIX_WIDTH, D_MODEL)), _mod_spec(B), full((1, D_MODEL)),
                  full((ROUTER_ROWS, D_MODEL)), full((ROUTER_ROWS, D_MODEL)), full((ROUTER_ROWS, 1))],
        out_specs=[row(D_MODEL), row(D_MODEL),
                   pl.BlockSpec((1, ROUTER_ROWS, ROW_TILE), lambda b_, i: (b_ * nt + i, 0, 0))],
        compiler_params=_cparams("arbitrary", "arbitrary"),
        name="out_proj_router",
    )(a, b, c, xs, w_out_bf16, mod, g2.reshape(1, D_MODEL), r_hi, r_lo, r_b)


def _route_kernel(lg_ref, info_ref, cnt_ref):
    lgt = lg_ref[0]
    tile = lgt.shape[1]
    lg = lgt[0:N_GROUPS]
    gmax = jnp.max(lg, axis=0, keepdims=True)
    ridx = lax.broadcasted_iota(jnp.int32, lg.shape, 0)
    g_idx = jnp.min(jnp.where(lg == gmax, ridx, N_GROUPS), axis=0, keepdims=True)
    p_grp = 1.0 / jnp.sum(jnp.exp(lg - gmax), axis=0, keepdims=True)
    le = jnp.zeros((EXPERTS_PER_GROUP, tile), F32)
    for gi in range(N_GROUPS):
        lo = ROUTER_EXPERT_ROW0 + gi * EXPERTS_PER_GROUP
        le = jnp.where(g_idx == gi, lgt[lo:lo + EXPERTS_PER_GROUP], le)
    ex = jnp.exp(le - jnp.max(le, axis=0, keepdims=True))
    pe = ex / jnp.sum(ex, axis=0, keepdims=True)
    ridx = lax.broadcasted_iota(jnp.int32, pe.shape, 0)
    m1 = jnp.max(pe, axis=0, keepdims=True)
    i1 = jnp.min(jnp.where(pe == m1, ridx, EXPERTS_PER_GROUP), axis=0, keepdims=True)
    pe2 = jnp.where(ridx == i1, -1.0, pe)
    m2 = jnp.max(pe2, axis=0, keepdims=True)
    i2 = jnp.min(jnp.where(pe2 == m2, ridx, EXPERTS_PER_GROUP), axis=0, keepdims=True)
    tot = m1 + m2
    w0 = p_grp * (m1 / tot)
    w1 = p_grp * (m2 / tot)
    e0 = g_idx * EXPERTS_PER_GROUP + i1
    e1 = g_idx * EXPERTS_PER_GROUP + i2
    eidx = lax.broadcasted_iota(jnp.int32, (N_EXPERTS, tile), 0)
    oh0 = jnp.where(eidx == e0, 1.0, 0.0)
    oh1 = jnp.where(eidx == e1, 1.0, 0.0)
    before = jnp.where(lax.broadcasted_iota(jnp.int32, (tile, tile), 0)
                       < lax.broadcasted_iota(jnp.int32, (tile, tile), 1), 1.0, 0.0).astype(BF16)
    c0 = jnp.dot(oh0.astype(BF16), before, preferred_element_type=F32)
    c1 = jnp.dot(oh1.astype(BF16), before, preferred_element_type=F32)
    cnt0 = jnp.sum(oh0, axis=1, keepdims=True)
    cnt = cnt0 + jnp.sum(oh1, axis=1, keepdims=True)
    rank0 = jnp.sum(oh0 * c0, axis=0, keepdims=True)
    rank1 = jnp.sum(oh1 * (c1 + cnt0), axis=0, keepdims=True)
    units = jnp.floor((cnt + (MOE_ALIGN - 1)) * (1.0 / MOE_ALIGN))
    lower = jnp.where(lax.broadcasted_iota(jnp.int32, (N_EXPERTS, N_EXPERTS), 1)
                      < lax.broadcasted_iota(jnp.int32, (N_EXPERTS, N_EXPERTS), 0), 1.0, 0.0).astype(BF16)
    loc = jnp.dot(lower, jnp.broadcast_to(units, (N_EXPERTS, tile)).astype(BF16),
                  preferred_element_type=F32) * MOE_ALIGN
    pos0 = jnp.sum(oh0 * loc, axis=0, keepdims=True) + rank0
    pos1 = jnp.sum(oh1 * loc, axis=0, keepdims=True) + rank1
    fields = [pos0, pos1, w0, w1, e0.astype(F32), e1.astype(F32), rank0, rank1]
    frow = lax.broadcasted_iota(jnp.int32, (len(fields), tile), 0)
    info = jnp.zeros((len(fields), tile), F32)
    for k, field in enumerate(fields):
        info = jnp.where(frow == k, field, info)
    info_ref[0] = info
    cnt_ref[0] = jnp.broadcast_to(cnt, (N_EXPERTS, 128))


def _route(logits_t):
    n_tiles = logits_t.shape[0]
    return pl.pallas_call(
        _route_kernel,
        out_shape=[jax.ShapeDtypeStruct((n_tiles, 8, ROW_TILE), F32),
                   jax.ShapeDtypeStruct((n_tiles, N_EXPERTS, 128), F32)],
        grid=(n_tiles,),
        in_specs=[pl.BlockSpec((1, ROUTER_ROWS, ROW_TILE), lambda t: (t, 0, 0))],
        out_specs=[pl.BlockSpec((1, 8, ROW_TILE), lambda t: (t, 0, 0)),
                   pl.BlockSpec((1, N_EXPERTS, 128), lambda t: (t, 0, 0))],
        compiler_params=_cparams("arbitrary"),
        name="moe_route",
    )(logits_t)


def _dispatch_kernel(dst_ref, loc_ref, nch_ref, zfill_ref, info_ref, h2_ref, out_ref,
                     xp_ref, zero_ref, sem, zsem):
    t = pl.program_id(0)
    nt = pl.num_programs(0)
    slot = t % 2
    info = info_ref[0]
    p = lax.broadcasted_iota(jnp.int32, (MOE_TILE_SLOTS, 1), 0).astype(F32)
    sel = jnp.where(p == info[0:1], 1.0, jnp.where(p == info[1:2], 1.0, 0.0)).astype(BF16)
    xp_ref[slot] = jnp.dot(sel, h2_ref[...].astype(BF16), preferred_element_type=F32)

    def copies(tt, sl, start):
        def per_expert(e, carry):
            base = tt * N_EXPERTS + e
            loc = loc_ref[base]
            dst = dst_ref[base]

            def per_chunk(j, carry2):
                cp = pltpu.make_async_copy(
                    xp_ref.at[sl, pl.ds(pl.multiple_of(loc + j * MOE_CAP, MOE_ALIGN), MOE_CAP)],
                    out_ref.at[pl.ds(pl.multiple_of(dst + j * MOE_CAP, MOE_ALIGN), MOE_CAP)],
                    sem.at[sl])
                if start:
                    cp.start()
                else:
                    cp.wait()
                return carry2

            return lax.fori_loop(0, nch_ref[base], per_chunk, carry)

        lax.fori_loop(0, N_EXPERTS, per_expert, 0)

    @pl.when(t > 0)
    def _():
        copies(t - 1, 1 - slot, False)

    @pl.when(t == 0)
    def _():
        zero_ref[...] = jnp.zeros_like(zero_ref)

        def zero_blocks(start):
            def per_block(j, carry):
                @pl.when(zfill_ref[j] == 1)
                def _():
                    cp = pltpu.make_async_copy(
                        zero_ref, out_ref.at[pl.ds(pl.multiple_of(j * MOE_BLOCK, MOE_BLOCK), MOE_BLOCK)], zsem)
                    if start:
                        cp.start()
                    else:
                        cp.wait()

                return carry

            lax.fori_loop(0, out_ref.shape[0] // MOE_BLOCK, per_block, 0)

        zero_blocks(True)
        zero_blocks(False)

    copies(t, slot, True)

    @pl.when(t == nt - 1)
    def _():
        copies(t, slot, False)


def _dispatch(tables, info, h2_flat, n_rows_static):
    n_tiles = info.shape[0]
    grid_spec = pltpu.PrefetchScalarGridSpec(
        num_scalar_prefetch=4,
        grid=(n_tiles,),
        in_specs=[pl.BlockSpec((1, 8, ROW_TILE), lambda t, *_: (t, 0, 0)),
                  pl.BlockSpec((ROW_TILE, D_MODEL), lambda t, *_: (t, 0))],
        out_specs=pl.BlockSpec(memory_space=pl.ANY),
        scratch_shapes=[pltpu.VMEM((2, MOE_TILE_SLOTS, D_MODEL), F32),
                        pltpu.VMEM((MOE_BLOCK, D_MODEL), F32),
                        pltpu.SemaphoreType.DMA((2,)),
                        pltpu.SemaphoreType.DMA(())],
    )
    return pl.pallas_call(
        _dispatch_kernel,
        out_shape=jax.ShapeDtypeStruct((n_rows_static, D_MODEL), F32),
        grid_spec=grid_spec,
        compiler_params=_cparams("arbitrary"),
        name="moe_dispatch",
    )(tables["dst"], tables["loc"], tables["nch"], tables["zfill"], info, h2_flat)


def _ffn_kernel(bexp_ref, used_ref, xblk_ref, x_ref, wg_ref, wu_ref, wd_ref, o_ref):
    del bexp_ref, xblk_ref
    j = pl.program_id(0)

    @pl.when(used_ref[j] == 1)
    def _():
        x = x_ref[...].astype(BF16)
        gate = jnp.dot(x, wg_ref[0].astype(BF16), preferred_element_type=F32)
        up = jnp.dot(x, wu_ref[0].astype(BF16), preferred_element_type=F32)
        hid = (gate * jax.nn.sigmoid(gate) * up).astype(BF16)
        o_ref[...] = jnp.dot(hid, wd_ref[0].astype(BF16), preferred_element_type=F32)

    @pl.when(used_ref[j] == 0)
    def _():
        o_ref[...] = jnp.zeros_like(o_ref)


def _expert_ffn(tables, xs_sorted, w_gate, w_up, w_down):
    P = xs_sorted.shape[0]
    grid_spec = pltpu.PrefetchScalarGridSpec(
        num_scalar_prefetch=3,
        grid=(P // MOE_BLOCK,),
        in_specs=[
            pl.BlockSpec((MOE_BLOCK, D_MODEL), lambda j, be, us, xb: (xb[j], 0)),
            pl.BlockSpec((1, D_MODEL, EXPERT_HIDDEN), lambda j, be, us, xb: (be[j], 0, 0)),
            pl.BlockSpec((1, D_MODEL, EXPERT_HIDDEN), lambda j, be, us, xb: (be[j], 0, 0)),
            pl.BlockSpec((1, EXPERT_HIDDEN, D_MODEL), lambda j, be, us, xb: (be[j], 0, 0)),
        ],
        out_specs=pl.BlockSpec((MOE_BLOCK, D_MODEL), lambda j, be, us, xb: (j, 0)),
    )
    return pl.pallas_call(
        _ffn_kernel,
        out_shape=jax.ShapeDtypeStruct((P, D_MODEL), F32),
        grid_spec=grid_spec,
        compiler_params=_cparams("arbitrary"),
        name="expert_ffn",
    )(tables["bexp"], tables["used"], tables["xblk"], xs_sorted, w_gate, w_up, w_down)


def _combine_kernel(dst_ref, nch_ref, nround_ref, infoc_ref, x_ref, mod_ref, ys_ref, o_ref,
                    yp_ref, acc_ref, sem):
    nt = pl.num_programs(1)
    tile = pl.program_id(0) * nt + pl.program_id(1)
    n_tiles = pl.num_programs(0) * nt
    slot = tile % 2

    def stage(tt, sl, rnd, start):
        def per_expert(e, carry):
            base = tt * N_EXPERTS + e

            @pl.when(nch_ref[base] > rnd)
            def _():
                cp = pltpu.make_async_copy(
                    ys_ref.at[pl.ds(pl.multiple_of(dst_ref[base] + rnd * MOE_CAP, MOE_ALIGN), MOE_CAP)],
                    yp_ref.at[sl, pl.ds(pl.multiple_of(e * MOE_CAP, MOE_ALIGN), MOE_CAP)],
                    sem.at[sl])
                if start:
                    cp.start()
                else:
                    cp.wait()

            return carry

        lax.fori_loop(0, N_EXPERTS, per_expert, 0)

    @pl.when(tile == 0)
    def _():
        yp_ref[...] = jnp.zeros_like(yp_ref)
        stage(0, 0, 0, True)

    @pl.when(tile + 1 < n_tiles)
    def _():
        stage(tile + 1, 1 - slot, 0, True)

    stage(tile, slot, 0, False)

    infoc = infoc_ref[0]
    w0, w1 = infoc[:, 2:3], infoc[:, 3:4]
    e0, e1 = infoc[:, 4:5], infoc[:, 5:6]
    r0, r1 = infoc[:, 6:7], infoc[:, 7:8]
    lane = lax.broadcasted_iota(jnp.int32, (1, N_EXPERTS * MOE_CAP), 1).astype(F32)

    def gathered(rnd):
        yp = yp_ref[slot].astype(BF16)

        def pick(e, r, w):
            rr = r - rnd * MOE_CAP
            ok = (rr >= 0.0) & (rr < MOE_CAP)
            sel = jnp.where(ok & (lane == e * MOE_CAP + rr), 1.0, 0.0).astype(BF16)
            return w * jnp.dot(sel, yp, preferred_element_type=F32)

        return pick(e0, r0, w0) + pick(e1, r1, w1)

    acc_ref[...] = gathered(0)

    def extra_round(rnd, carry):
        stage(tile, slot, rnd, True)
        stage(tile, slot, rnd, False)
        acc_ref[...] += gathered(rnd.astype(F32))
        return carry

    lax.fori_loop(1, nround_ref[tile], extra_round, 0)
    gate2 = mod_ref[0][:, 5 * D_MODEL:6 * D_MODEL]
    o_ref[0] = x_ref[0] + gate2 * acc_ref[...]


def _combine(tables, info_cols, xs, mod, ys):
    B, N, _ = xs.shape
    nt = N // ROW_TILE
    grid_spec = pltpu.PrefetchScalarGridSpec(
        num_scalar_prefetch=3,
        grid=(B, nt),
        in_specs=[pl.BlockSpec((1, ROW_TILE, 8), lambda b, i, *_: (b * nt + i, 0, 0)),
                  pl.BlockSpec((1, ROW_TILE, D_MODEL), lambda b, i, *_: (b, i, 0)),
                  pl.BlockSpec((1, 1, 6 * D_MODEL), lambda b, i, *_: (jnp.where(i == 0, B, b), 0, 0)),
                  pl.BlockSpec(memory_space=pl.ANY)],
        out_specs=pl.BlockSpec((1, ROW_TILE, D_MODEL), lambda b, i, *_: (b, i, 0)),
        scratch_shapes=[pltpu.VMEM((2, N_EXPERTS * MOE_CAP, D_MODEL), F32),
                        pltpu.VMEM((ROW_TILE, D_MODEL), F32),
                        pltpu.SemaphoreType.DMA((2,))],
    )
    return pl.pallas_call(
        _combine_kernel,
        out_shape=jax.ShapeDtypeStruct((B, N, D_MODEL), F32),
        grid_spec=grid_spec,
        compiler_params=_cparams("arbitrary", "arbitrary"),
        name="moe_combine",
    )(tables["dst"], tables["nch"], tables["nround"], info_cols, xs, mod, ys)


def _moe_static_rows(n_tokens):
    n_tiles = n_tokens // ROW_TILE
    worst = n_tokens * TOP_K + n_tiles * N_EXPERTS * (MOE_ALIGN - 1) + N_EXPERTS * (2 * MOE_BLOCK - 1)
    return -(-worst // MOE_BLOCK) * MOE_BLOCK


def _moe_tables(counts, n_rows_static):
    n_tiles = counts.shape[0]
    cnt = counts.astype(jnp.int32)
    cnt_al = -(-cnt // MOE_ALIGN) * MOE_ALIGN
    seg_off = jnp.cumsum(cnt_al, axis=0) - cnt_al
    tot = jnp.sum(cnt_al, axis=0)
    nblk = -(-tot // MOE_BLOCK)
    esize = (nblk + 1) * MOE_BLOCK
    eend = jnp.cumsum(esize)
    estart = eend - esize
    loc = jnp.cumsum(cnt_al, axis=1) - cnt_al
    nch = -(-cnt_al // MOE_CAP)
    blk0 = jnp.arange(n_rows_static // MOE_BLOCK, dtype=jnp.int32) * MOE_BLOCK
    bexp = jnp.minimum(jnp.searchsorted(eend, blk0, side="right"), N_EXPERTS - 1).astype(jnp.int32)
    used = ((blk0 - estart[bexp]) < nblk[bexp] * MOE_BLOCK) & (blk0 < eend[-1])
    own = jnp.where(used, jnp.arange(blk0.shape[0], dtype=jnp.int32), -1)
    xblk = jnp.maximum(lax.cummax(own, axis=0), jnp.argmax(used).astype(jnp.int32))
    i32 = lambda a: a.astype(jnp.int32)
    return {
        "dst": i32((estart[None, :] + seg_off).reshape(-1)),
        "loc": i32(loc.reshape(-1)),
        "nch": i32(nch.reshape(-1)),
        "nround": i32(jnp.maximum(jnp.max(nch, axis=1), 1)),
        "zfill": i32(~used | ((blk0 - estart[bexp]) == (nblk[bexp] - 1) * MOE_BLOCK)),
        "bexp": bexp, "used": i32(used), "xblk": i32(xblk),
    }


def _rope_tables(n_ctx, n_lat):
    row = jnp.repeat(jnp.arange(n_lat // GRID_W, dtype=F32), GRID_W)
    col = jnp.tile(jnp.arange(GRID_W, dtype=F32), n_lat // GRID_W)
    inv = ROPE_BASE ** (-jnp.arange(0, ROPE_AXIS_DIM, 2, dtype=F32) / ROPE_AXIS_DIM)
    ang = jnp.concatenate([row[:, None] * inv, col[:, None] * inv], axis=-1)
    cos = jnp.concatenate([jnp.ones((n_ctx, ROPE_AXIS_DIM), F32), jnp.cos(ang)], axis=0)
    sin = jnp.concatenate([jnp.zeros((n_ctx, ROPE_AXIS_DIM), F32), jnp.sin(ang)], axis=0)
    return jnp.tile(cos, (1, 4)), jnp.tile(jnp.concatenate([-sin, sin], axis=1), (1, 2))


def _block_diag(w):
    g, n, _ = w.shape
    out = jnp.zeros((g * n, g * n), w.dtype)
    for i in range(g):
        out = out.at[i * n:(i + 1) * n, i * n:(i + 1) * n].set(w[i])
    return out


def kernel(x, c, ctx, c_ctx, norm1_g, norm2_g, ada_w, ada_b, w_in, w_out, pool_w, pool_scale,
           q_norm_g, k_norm_g, attn_sink, hgrn_lb, hgrn_norm_g, router_group_w, router_group_b,
           router_expert_w, router_expert_b, expert_w_gate, expert_w_up, expert_w_down):
    B, L, _ = x.shape
    C = ctx.shape[1]
    N = C + L
    depth = ada_w.shape[0]
    assert C == ROW_TILE and L % ROW_TILE == 0 and L % GRID_W == 0

    cos2, sin2 = _rope_tables(C, L)
    sm = jax.nn.softmax(hgrn_lb.astype(F32), axis=0)
    lb_all = jnp.cumsum(sm, axis=0) - sm[0]

    craw = jnp.concatenate([c, c_ctx[None], jnp.zeros((8 - B - 1, D_MODEL), F32)], axis=0)
    mod_all = _ada(craw, ada_w, ada_b)[:, :B + 1].reshape(depth, B + 1, 1, 6 * D_MODEL)

    xs = jnp.concatenate([ctx, x], axis=1)
    for l in range(depth):
        mod = mod_all[l]
        u, q, k, v, hq, hi, hzf, hzb, hgate = _in_proj(xs, norm1_g[l], mod, w_in[l].astype(BF16))
        a = _pool(u, _block_diag(pool_w[l]).astype(BF16), pool_scale[l], C)
        qh, kh, vh = _qk_prep(q, k, v, cos2, sin2,
                              jnp.tile(q_norm_g[l].reshape(1, HEAD_DIM), (1, 2)),
                              jnp.tile(k_norm_g[l].reshape(1, HEAD_DIM), (1, 2)))
        b_mix = _attention(qh, kh, vh, attn_sink[l], C)
        o_f = _hgrn_scan(hq, hi, hzf, lb_all[l, 0], reverse=False)
        c_mix = _hgrn_scan(hq, hi, hzb, lb_all[l, 1], reverse=True, o_fwd=o_f, hgate=hgate,
                           norm_g=hgrn_norm_g[l])
        pad_rows = lambda n: jnp.zeros((n, D_MODEL), F32)
        r_w = jnp.concatenate([router_group_w[l].T, pad_rows(ROUTER_EXPERT_ROW0 - N_GROUPS),
                               router_expert_w[l].T,
                               pad_rows(ROUTER_ROWS - ROUTER_EXPERT_ROW0 - N_EXPERTS)], axis=0)
        r_b = jnp.concatenate([router_group_b[l], jnp.zeros((ROUTER_EXPERT_ROW0 - N_GROUPS,), F32),
                               router_expert_b[l],
                               jnp.zeros((ROUTER_ROWS - ROUTER_EXPERT_ROW0 - N_EXPERTS,), F32)])
        r_hi = r_w.astype(BF16)
        r_lo = (r_w - r_hi.astype(F32)).astype(BF16)
        xs, h2, logits_t = _out_proj(a, b_mix, c_mix, xs, w_out[l].astype(BF16), mod, norm2_g[l],
                                     r_hi, r_lo, r_b.reshape(ROUTER_ROWS, 1))
        info, counts = _route(logits_t)
        n_rows_static = _moe_static_rows(B * N)
        tables = _moe_tables(counts[:, :, 0], n_rows_static)
        xs_sorted = _dispatch(tables, info, h2.reshape(B * N, D_MODEL), n_rows_static)
        ys = _expert_ffn(tables, xs_sorted, expert_w_gate[l], expert_w_up[l], expert_w_down[l])
        xs = _combine(tables, jnp.swapaxes(info, 1, 2), xs, mod, ys)
    return xs[:, C:]
```

```python
import functools

import numpy as np
import jax
import jax.numpy as jnp
from jax import lax
from jax.experimental import pallas as pl
from jax.experimental.pallas import tpu as pltpu

D_MODEL = 1024
DEPTH = 4
GRID_W = 64
NORM_EPS = 1e-6
NEG_INF = -1e30

POOL_WINDOWS = (2, 4, 8, 16)
POOL_GROUP_DIM = 64
POOL_DIM = 256

HEAD_DIM = 64
ATTN_HEADS = 8
ATTN_KV_HEADS = 2
Q_PER_KV = 4
ATTN_DIM = 512
KV_DIM = 128
ATTN_WINDOW = 128
ATTN_BLOCK = 128
ROPE_BASE = 10000.0
ROPE_AXIS_DIM = 32

HG_HEADS = 4
HG_DIM = 64
HG_WIDTH = 256
HG_CHUNK = 64
HG_LEVELS = 7

MIX_WIDTH = 1024
IN_SIZES = (POOL_DIM, ATTN_DIM, KV_DIM, KV_DIM, HG_WIDTH, HG_WIDTH, HG_WIDTH, HG_WIDTH, HG_WIDTH)
IN_OFFS = tuple(int(sum(IN_SIZES[:i])) for i in range(len(IN_SIZES) + 1))
IN_COLS = IN_OFFS[-1]

N_GROUPS = 4
EXPERTS_PER_GROUP = 8
N_EXPERTS = 32
TOP_K = 2
EXPERT_HIDDEN = 512
MOE_BLOCK = 256
ROUTER_ROWS = 128
ROUTER_EXPERT_ROW0 = 8
MOE_ALIGN = 8
MOE_CAP = 32

ROW_TILE = 256
MOE_TILE_SLOTS = TOP_K * ROW_TILE + N_EXPERTS * MOE_ALIGN
ADA_COL_TILE = 1536
VMEM_LIMIT = 56 * 1024 * 1024

F32 = jnp.float32
BF16 = jnp.bfloat16


def _cparams(*sem):
    return pltpu.CompilerParams(dimension_semantics=sem, vmem_limit_bytes=VMEM_LIMIT)


def _rms_scale(x):
    return lax.rsqrt(jnp.mean(x * x, axis=-1, keepdims=True) + NORM_EPS)


def _ada_kernel(c_ref, w_ref, b_ref, o_ref):
    s = c_ref[...]
    s = (s * jax.nn.sigmoid(s)).astype(BF16)
    o_ref[0] = jnp.dot(s, w_ref[0].astype(BF16), preferred_element_type=F32) + b_ref[0]


def _ada(craw, ada_w, ada_b):
    depth = ada_w.shape[0]
    ncol = ada_w.shape[2]
    return pl.pallas_call(
        _ada_kernel,
        out_shape=jax.ShapeDtypeStruct((depth, 8, ncol), F32),
        grid=(depth, ncol // ADA_COL_TILE),
        in_specs=[
            pl.BlockSpec((8, D_MODEL), lambda l, j: (0, 0)),
            pl.BlockSpec((1, D_MODEL, ADA_COL_TILE), lambda l, j: (l, 0, j)),
            pl.BlockSpec((1, 1, ADA_COL_TILE), lambda l, j: (l, 0, j)),
        ],
        out_specs=pl.BlockSpec((1, 8, ADA_COL_TILE), lambda l, j: (l, 0, j)),
        compiler_params=_cparams("arbitrary", "arbitrary"),
        name="ada_mod",
    )(craw, ada_w, ada_b.reshape(depth, 1, ncol))


def _in_proj_kernel(x_ref, g_ref, mod_ref, w_ref, *out_refs):
    x = x_ref[0]
    mod = mod_ref[0]
    sh = mod[:, 0:D_MODEL]
    sc = mod[:, D_MODEL:2 * D_MODEL]
    h = (x * _rms_scale(x) * g_ref[...] * (1.0 + sc) + sh).astype(BF16)
    for o_ref, lo, hi in zip(out_refs, IN_OFFS[:-1], IN_OFFS[1:]):
        o_ref[0] = jnp.dot(h, w_ref[:, lo:hi], preferred_element_type=F32)


def _mod_spec(n_batch):
    return pl.BlockSpec((1, 1, 6 * D_MODEL), lambda b, i: (jnp.where(i == 0, n_batch, b), 0, 0))


def _in_proj(xs, g1, mod, w_in_bf16):
    B, N, _ = xs.shape
    nt = N // ROW_TILE
    return pl.pallas_call(
        _in_proj_kernel,
        out_shape=[jax.ShapeDtypeStruct((B, N, w), F32) for w in IN_SIZES],
        grid=(B, nt),
        in_specs=[
            pl.BlockSpec((1, ROW_TILE, D_MODEL), lambda b, i: (b, i, 0)),
            pl.BlockSpec((1, D_MODEL), lambda b, i: (0, 0)),
            _mod_spec(B),
            pl.BlockSpec((D_MODEL, IN_COLS), lambda b, i: (0, 0)),
        ],
        out_specs=[pl.BlockSpec((1, ROW_TILE, w), lambda b, i: (b, i, 0)) for w in IN_SIZES],
        compiler_params=_cparams("arbitrary", "arbitrary"),
        name="in_proj",
    )(xs, g1.reshape(1, D_MODEL), mod, w_in_bf16)


def _qk_prep_kernel(q_ref, k_ref, v_ref, cos_ref, sin_ref, gq_ref, gk_ref, qh_ref, kh_ref, vh_ref):
    cos = cos_ref[...]
    sin = sin_ref[...]
    lane = lax.broadcasted_iota(jnp.int32, (1, 2 * HEAD_DIM), 1)
    lo_head = lane < HEAD_DIM
    first_half = (lane % HEAD_DIM) < (HEAD_DIM // 2)

    def norm_rope(x2, g, scale):
        sq = x2 * x2
        s0 = jnp.sum(jnp.where(lo_head, sq, 0.0), axis=-1, keepdims=True)
        s1 = jnp.sum(jnp.where(lo_head, 0.0, sq), axis=-1, keepdims=True)
        ms = jnp.where(lo_head, s0, s1) * (1.0 / HEAD_DIM)
        xn = x2 * lax.rsqrt(ms + NORM_EPS) * g
        swapped = jnp.where(first_half, pltpu.roll(xn, 2 * HEAD_DIM - HEAD_DIM // 2, 1),
                            pltpu.roll(xn, HEAD_DIM // 2, 1))
        return (xn * cos + swapped * sin) * scale

    for p in range(ATTN_HEADS // 2):
        y = norm_rope(q_ref[0, :, 2 * HEAD_DIM * p:2 * HEAD_DIM * (p + 1)], gq_ref[...], HEAD_DIM ** -0.5)
        qh_ref[0, 2 * p] = y[:, :HEAD_DIM].astype(BF16)
        qh_ref[0, 2 * p + 1] = y[:, HEAD_DIM:].astype(BF16)
    y = norm_rope(k_ref[0], gk_ref[...], 1.0)
    kh_ref[0, 0] = y[:, :HEAD_DIM].astype(BF16)
    kh_ref[0, 1] = y[:, HEAD_DIM:].astype(BF16)
    v = v_ref[0]
    vh_ref[0, 0] = v[:, :HEAD_DIM].astype(BF16)
    vh_ref[0, 1] = v[:, HEAD_DIM:].astype(BF16)


def _qk_prep(q, k, v, cos2, sin2, gq2, gk2):
    B, N, _ = q.shape
    nt = N // ROW_TILE
    row = lambda w: pl.BlockSpec((1, ROW_TILE, w), lambda b, i: (b, i, 0))
    tab = pl.BlockSpec((ROW_TILE, 2 * HEAD_DIM), lambda b, i: (i, 0))
    vec = pl.BlockSpec((1, 2 * HEAD_DIM), lambda b, i: (0, 0))
    head = lambda n: pl.BlockSpec((1, n, ROW_TILE, HEAD_DIM), lambda b, i: (b, 0, i, 0))
    return pl.pallas_call(
        _qk_prep_kernel,
        out_shape=[jax.ShapeDtypeStruct((B, ATTN_HEADS, N, HEAD_DIM), BF16),
                   jax.ShapeDtypeStruct((B, ATTN_KV_HEADS, N, HEAD_DIM), BF16),
                   jax.ShapeDtypeStruct((B, ATTN_KV_HEADS, N, HEAD_DIM), BF16)],
        grid=(B, nt),
        in_specs=[row(ATTN_DIM), row(KV_DIM), row(KV_DIM), tab, tab, vec, vec],
        out_specs=[head(ATTN_HEADS), head(ATTN_KV_HEADS), head(ATTN_KV_HEADS)],
        compiler_params=_cparams("arbitrary", "arbitrary"),
        name="qk_prep",
    )(q, k, v, cos2, sin2, gq2, gk2)


def _attn_kernel(sink_ref, q_ref, kp_ref, kc_ref, kn_ref, kx_ref, vp_ref, vc_ref, vn_ref, vx_ref,
                 o_ref, *, n_ctx, n_rows):
    i = pl.program_id(1)
    blk = ATTN_BLOCK
    t = i * blk + lax.broadcasted_iota(jnp.int32, (blk, 1), 0)
    kr = (i - 1) * blk + lax.broadcasted_iota(jnp.int32, (1, 3 * blk), 1)
    ok_loc = (t >= n_ctx) & (kr >= n_ctx) & (kr < n_rows) & (jnp.abs(t - kr) <= ATTN_WINDOW)
    bias = jnp.concatenate([jnp.where(ok_loc, 0.0, NEG_INF), jnp.zeros((blk, n_ctx), F32)],
                           axis=1)[None]
    heads = []
    for g in range(ATTN_KV_HEADS):
        q = q_ref[0, Q_PER_KV * g:Q_PER_KV * (g + 1)].reshape(Q_PER_KV * blk, HEAD_DIM)
        k = jnp.concatenate([kp_ref[0, g], kc_ref[0, g], kn_ref[0, g], kx_ref[0, g]], axis=0)
        v = jnp.concatenate([vp_ref[0, g], vc_ref[0, g], vn_ref[0, g], vx_ref[0, g]], axis=0)
        s = lax.dot_general(q, k, (((1,), (1,)), ((), ())), preferred_element_type=F32)
        nk = s.shape[-1]
        s = (s.reshape(Q_PER_KV, blk, nk) + bias).reshape(Q_PER_KV * blk, nk)
        sk = jnp.concatenate(
            [jnp.full((blk, 1), sink_ref[Q_PER_KV * g + hh], F32) for hh in range(Q_PER_KV)], axis=0)
        m = jnp.maximum(jnp.max(s, axis=-1, keepdims=True), sk)
        p = jnp.exp(s - m)
        denom = jnp.sum(p, axis=-1, keepdims=True) + jnp.exp(sk - m)
        o = jnp.dot(p.astype(BF16), v, preferred_element_type=F32) / denom
        heads += [o[hh * blk:(hh + 1) * blk] for hh in range(Q_PER_KV)]
    o_ref[0] = jnp.concatenate(heads, axis=1).astype(BF16)


def _attention(qh, kh, vh, sink, n_ctx):
    B, _, N, _ = qh.shape
    nq = N // ATTN_BLOCK
    kv = lambda f: pl.BlockSpec((1, ATTN_KV_HEADS, ATTN_BLOCK, HEAD_DIM), f)
    prev = lambda b, i: (b, 0, jnp.maximum(i - 1, 0), 0)
    cur = lambda b, i: (b, 0, i, 0)
    nxt = lambda b, i: (b, 0, jnp.minimum(i + 1, nq - 1), 0)
    ctx = pl.BlockSpec((1, ATTN_KV_HEADS, n_ctx, HEAD_DIM), lambda b, i: (b, 0, 0, 0))
    return pl.pallas_call(
        functools.partial(_attn_kernel, n_ctx=n_ctx, n_rows=N),
        out_shape=jax.ShapeDtypeStruct((B, N, ATTN_DIM), BF16),
        grid=(B, nq),
        in_specs=[pl.BlockSpec(memory_space=pltpu.SMEM),
                  pl.BlockSpec((1, ATTN_HEADS, ATTN_BLOCK, HEAD_DIM), cur),
                  kv(prev), kv(cur), kv(nxt), ctx, kv(prev), kv(cur), kv(nxt), ctx],
        out_specs=pl.BlockSpec((1, ATTN_BLOCK, ATTN_DIM), lambda b, i: (b, i, 0)),
        compiler_params=_cparams("arbitrary", "arbitrary"),
        name="band_attention",
    )(sink, qh, kh, kh, kh, kh, vh, vh, vh, vh)


def _hgrn_tables(reverse):
    c = HG_CHUNK
    t = np.arange(c)
    msum = np.zeros((HG_LEVELS + 1, c, c), np.float32)
    qm = np.zeros((HG_LEVELS, c, 1), np.float32)
    km = np.zeros((HG_LEVELS, c, 1), np.float32)
    sm = np.zeros((HG_LEVELS, c, c), np.float32)
    qm[0] = 1.0
    km[0] = 1.0
    sm[0] = np.eye(c)
    for j in range(1, HG_LEVELS):
        m = c >> (j - 1)
        half = m // 2
        later = (t % m) >= half
        mid = (t // m) * m + half
        r = t[None, :]
        rows_later = later[:, None] & (r > mid[:, None]) & (r <= t[:, None])
        rows_early = (~later)[:, None] & (r > t[:, None]) & (r <= mid[:, None])
        msum[j - 1] = (rows_later | rows_early).astype(np.float32)
        qm[j, :, 0] = later
        km[j, :, 0] = ~later
        sm[j] = ((t[:, None] // m) == (t[None, :] // m)).astype(np.float32)
    msum[HG_LEVELS - 1] = (t[None, :] <= t[:, None]).astype(np.float32)
    msum[HG_LEVELS] = (t[None, :] > t[:, None]).astype(np.float32)
    if reverse:
        msum = msum[:, ::-1, ::-1]
        qm = qm[:, ::-1]
        km = km[:, ::-1]
        sm = sm[:, ::-1, ::-1]
    ones = np.ones((1, 1, HG_WIDTH), np.float32)
    return (jnp.asarray(msum.reshape(-1, c), BF16),
            jnp.asarray(qm * ones), jnp.asarray(km * ones),
            jnp.asarray(np.tile(sm, (1, 1, HG_HEADS))))


def _head_mask():
    h = np.arange(HG_WIDTH) // HG_DIM
    return (h[:, None] == h[None, :]).astype(np.float32)


def _hgrn_kernel(*refs, reverse):
    if reverse:
        (hq_ref, hi_ref, hz_ref, lb_ref, ms_ref, qm_ref, km_ref, sm_ref, hm_ref,
         of_ref, hg_ref, hmean_ref, ng_ref, o_ref, st_ref) = refs
    else:
        (hq_ref, hi_ref, hz_ref, lb_ref, ms_ref, qm_ref, km_ref, sm_ref, hm_ref,
         o_ref, st_ref) = refs
    c = HG_CHUNK
    n_chunks = ROW_TILE // c

    @pl.when(pl.program_id(1) == 0)
    def _():
        st_ref[...] = jnp.zeros_like(st_ref)

    lb = lb_ref[...]
    hm_bf = hm_ref[...]
    hm = hm_bf.astype(F32)
    outs = [None] * n_chunks
    order = range(n_chunks - 1, -1, -1) if reverse else range(n_chunks)
    for ci in order:
        rows = slice(ci * c, (ci + 1) * c)
        hq = hq_ref[0, rows, :]
        q = hq * jax.nn.sigmoid(hq)
        v = hi_ref[0, rows, :]
        z = hz_ref[0, rows, :]
        g = jnp.log(lb + (1.0 - lb) * jax.nn.sigmoid(z))
        kk = (1.0 - lb) * jax.nn.sigmoid(-z)
        g_hi = g.astype(BF16)
        g_lo = (g - g_hi.astype(F32)).astype(BF16)
        dsum = jnp.dot(ms_ref[...], jnp.concatenate([g_hi, g_lo], axis=1), preferred_element_type=F32)
        e = jnp.exp(dsum[:, :HG_WIDTH] + dsum[:, HG_WIDTH:])
        scores = jnp.zeros((c, HG_HEADS * c), F32)
        for lv in range(HG_LEVELS):
            if lv == 0:
                ql, kl = q, kk
            else:
                el = e[(lv - 1) * c:lv * c]
                ql = q * el * qm_ref[lv]
                kl = kk * el * km_ref[lv]
            kbd = jnp.concatenate([kl.astype(BF16)] * HG_HEADS, axis=0) * hm_bf
            s_l = lax.dot_general(ql.astype(BF16), kbd, (((1,), (1,)), ((), ())),
                                  preferred_element_type=F32)
            scores = scores + s_l * sm_ref[lv]
        vbd = jnp.concatenate([v.astype(BF16)] * HG_HEADS, axis=0) * hm_bf
        o = jnp.dot(scores.astype(BF16), vbd, preferred_element_type=F32)
        e_cum = e[(HG_LEVELS - 1) * c:HG_LEVELS * c]
        st = st_ref[...]
        o = o + lax.dot_general((q * e_cum).astype(BF16), st.astype(BF16), (((1,), (1,)), ((), ())),
                                preferred_element_type=F32)
        k_dec = kk * e[HG_LEVELS * c:(HG_LEVELS + 1) * c]
        e_end = e_cum[0:1] if reverse else e_cum[c - 1:c]
        upd = jnp.dot(v.T.astype(BF16), k_dec.astype(BF16), preferred_element_type=F32)
        st_ref[...] = st * e_end + upd * hm
        outs[ci] = o
    o_all = jnp.concatenate(outs, axis=0)
    if reverse:
        o_sum = of_ref[0] + o_all
        ms = jnp.dot((o_sum * o_sum).astype(BF16), hmean_ref[...], preferred_element_type=F32)
        gate = hg_ref[0]
        o_ref[0] = (o_sum * lax.rsqrt(ms + NORM_EPS) * ng_ref[...]
                    * (gate * jax.nn.sigmoid(gate))).astype(BF16)
    else:
        o_ref[0] = o_all


def _hgrn_scan(hq, hi, hz, lb, reverse, o_fwd=None, hgate=None, norm_g=None):
    B, N, _ = hq.shape
    nt = N // ROW_TILE
    if reverse:
        order = lambda b, i: (b, jnp.where(i == 0, 0, nt - i), 0)
    else:
        order = lambda b, i: (b, i, 0)
    row = pl.BlockSpec((1, ROW_TILE, HG_WIDTH), order)
    const = lambda a: pl.BlockSpec(a.shape, lambda b, i: (0,) * a.ndim)
    msum, qm, km, sm = _hgrn_tables(reverse)
    hm = jnp.asarray(_head_mask(), BF16)
    args = [hq, hi, hz, lb.reshape(1, HG_WIDTH), msum, qm, km, sm, hm]
    specs = [row, row, row] + [const(a) for a in args[3:]]
    if reverse:
        hmean = jnp.asarray(_head_mask() / HG_DIM, BF16)
        ng = jnp.tile(norm_g.reshape(1, HG_DIM), (1, HG_HEADS))
        args += [o_fwd, hgate, hmean, ng]
        specs += [row, row, const(hmean), const(ng)]
    return pl.pallas_call(
        functools.partial(_hgrn_kernel, reverse=reverse),
        out_shape=jax.ShapeDtypeStruct((B, N, HG_WIDTH), BF16 if reverse else F32),
        grid=(B, nt),
        in_specs=specs,
        out_specs=row,
        scratch_shapes=[pltpu.VMEM((HG_WIDTH, HG_WIDTH), F32)],
        compiler_params=_cparams("arbitrary", "arbitrary"),
        name="hgrn_bwd" if reverse else "hgrn_fwd",
    )(*args)


POOL_HALO = 8


def _pool_kernel(up_ref, u_ref, un_ref, w_ref, sc_ref, o_ref, e_ref, *, n_ctx, n_lat):
    i = pl.program_id(1)
    nt = pl.num_programs(1)
    u = u_ref[0]
    has_prev = i >= 2
    has_next = (i >= 1) & (i <= nt - 2)
    e_ref[0:POOL_HALO, :] = jnp.where(has_prev, up_ref[0], 0.0)
    e_ref[POOL_HALO:POOL_HALO + ROW_TILE, :] = u
    e_ref[POOL_HALO + ROW_TILE:, :] = jnp.where(has_next, un_ref[0], 0.0)
    seg_start = jnp.where(i == 0, 0, n_ctx)
    seg_len = jnp.where(i == 0, n_ctx, n_lat)
    tau = i * ROW_TILE - seg_start + lax.broadcasted_iota(jnp.int32, (ROW_TILE, 1), 0)
    lane_grp = lax.broadcasted_iota(jnp.int32, (1, 2 * POOL_GROUP_DIM), 1) // POOL_GROUP_DIM
    halves = []
    for hb in range(2):
        w_a, w_b = POOL_WINDOWS[2 * hb], POOL_WINDOWS[2 * hb + 1]
        lanes = slice(2 * POOL_GROUP_DIM * hb, 2 * POOL_GROUP_DIM * (hb + 1))
        acc = jnp.zeros((ROW_TILE, 2 * POOL_GROUP_DIM), F32)
        for off in range(-((w_b - 1) // 2), w_b // 2 + 1):
            x = e_ref[POOL_HALO + off:POOL_HALO + off + ROW_TILE, lanes]
            if -((w_a - 1) // 2) <= off <= w_a // 2:
                acc = acc + x
            else:
                acc = acc + jnp.where(lane_grp == 1, x, 0.0)

        def count(w):
            lo = jnp.maximum(tau - (w - 1) // 2, 0)
            hi = jnp.minimum(tau + w // 2 + 1, seg_len)
            return (hi - lo).astype(F32)

        cnt = jnp.where(lane_grp == 0, count(w_a), count(w_b))
        halves.append(acc / cnt)
    pooled = jnp.concatenate(halves, axis=1) - u
    y = jnp.dot(pooled.astype(BF16), w_ref[...], preferred_element_type=F32) * sc_ref[...]
    o_ref[0] = y.astype(BF16)


def _pool(u, w_bd, scale, n_ctx):
    B, N, _ = u.shape
    nt = N // ROW_TILE
    per = ROW_TILE // POOL_HALO
    nh = N // POOL_HALO
    return pl.pallas_call(
        functools.partial(_pool_kernel, n_ctx=n_ctx, n_lat=N - n_ctx),
        out_shape=jax.ShapeDtypeStruct((B, N, POOL_DIM), BF16),
        grid=(B, nt),
        in_specs=[
            pl.BlockSpec((1, POOL_HALO, POOL_DIM), lambda b, i: (b, jnp.maximum(i * per - 1, 0), 0)),
            pl.BlockSpec((1, ROW_TILE, POOL_DIM), lambda b, i: (b, i, 0)),
            pl.BlockSpec((1, POOL_HALO, POOL_DIM), lambda b, i: (b, jnp.minimum((i + 1) * per, nh - 1), 0)),
            pl.BlockSpec((POOL_DIM, POOL_DIM), lambda b, i: (0, 0)),
            pl.BlockSpec((1, POOL_DIM), lambda b, i: (0, 0)),
        ],
        out_specs=pl.BlockSpec((1, ROW_TILE, POOL_DIM), lambda b, i: (b, i, 0)),
        scratch_shapes=[pltpu.VMEM((ROW_TILE + 2 * POOL_HALO, POOL_DIM), F32)],
        compiler_params=_cparams("arbitrary", "arbitrary"),
        name="pool_mixer",
    )(u, u, u, w_bd, scale.reshape(1, POOL_DIM))


def _out_proj_kernel(a_ref, b_ref, c_ref, x_ref, w_ref, mod_ref, g2_ref, rh_ref, rl_ref, rb_ref,
                     xo_ref, h2_ref, lg_ref):
    mix = (jnp.dot(a_ref[0], w_ref[0:POOL_DIM, :], preferred_element_type=F32)
           + jnp.dot(b_ref[0], w_ref[POOL_DIM:POOL_DIM + ATTN_DIM, :], preferred_element_type=F32)
           + jnp.dot(c_ref[0], w_ref[POOL_DIM + ATTN_DIM:, :], preferred_element_type=F32))
    mod = mod_ref[0]
    gate1 = mod[:, 2 * D_MODEL:3 * D_MODEL]
    sh2 = mod[:, 3 * D_MODEL:4 * D_MODEL]
    sc2 = mod[:, 4 * D_MODEL:5 * D_MODEL]
    x = x_ref[0] + gate1 * mix
    xo_ref[0] = x
    h2 = x * _rms_scale(x) * g2_ref[...] * (1.0 + sc2) + sh2
    h2_ref[0] = h2
    h_hi = h2.astype(BF16)
    h_lo = (h2 - h_hi.astype(F32)).astype(BF16)
    lg = (jnp.dot(h_hi, rh_ref[...], preferred_element_type=F32)
          + jnp.dot(h_lo, rh_ref[...], preferred_element_type=F32)
          + jnp.dot(h_hi, rl_ref[...], preferred_element_type=F32) + rb_ref[...])
    lg_ref[0] = lg.T


def _out_proj(a, b, c, xs, w_out_bf16, mod, g2, r_hi, r_lo, r_b):
    B, N, _ = xs.shape
    nt = N // ROW_TILE
    row = lambda w: pl.BlockSpec((1, ROW_TILE, w), lambda b_, i: (b_, i, 0))
    full = lambda s: pl.BlockSpec(s, lambda b_, i: (0,) * len(s))
    return pl.pallas_call(
        _out_proj_kernel,
        out_shape=[jax.ShapeDtypeStruct((B, N, D_MODEL), F32),
                   jax.ShapeDtypeStruct((B, N, D_MODEL), F32),
                   jax.ShapeDtypeStruct((B * nt, ROUTER_ROWS, ROW_TILE), F32)],
        grid=(B, nt),
        in_specs=[row(POOL_DIM), row(ATTN_DIM), row(HG_WIDTH), row(D_MODEL),
                  full((MIX_WIDTH, D_MODEL)), _mod_spec(B), full((1, D_MODEL)),
                  full((D_MODEL, ROUTER_ROWS)), full((D_MODEL, ROUTER_ROWS)), full((1, ROUTER_ROWS))],
        out_specs=[row(D_MODEL), row(D_MODEL),
                   pl.BlockSpec((1, ROUTER_ROWS, ROW_TILE), lambda b_, i: (b_ * nt + i, 0, 0))],
        compiler_params=_cparams("arbitrary", "arbitrary"),
        name="out_proj_router",
    )(a, b, c, xs, w_out_bf16, mod, g2.reshape(1, D_MODEL), r_hi, r_lo, r_b)


def _route_kernel(lg_ref, info_ref, cnt_ref):
    lgt = lg_ref[0]
    tile = lgt.shape[1]
    lg = lgt[0:N_GROUPS]
    gmax = jnp.max(lg, axis=0, keepdims=True)
    ridx = lax.broadcasted_iota(jnp.int32, lg.shape, 0)
    g_idx = jnp.min(jnp.where(lg == gmax, ridx, N_GROUPS), axis=0, keepdims=True)
    p_grp = 1.0 / jnp.sum(jnp.exp(lg - gmax), axis=0, keepdims=True)
    le = jnp.zeros((EXPERTS_PER_GROUP, tile), F32)
    for gi in range(N_GROUPS):
        lo = ROUTER_EXPERT_ROW0 + gi * EXPERTS_PER_GROUP
        le = jnp.where(g_idx == gi, lgt[lo:lo + EXPERTS_PER_GROUP], le)
    ex = jnp.exp(le - jnp.max(le, axis=0, keepdims=True))
    pe = ex / jnp.sum(ex, axis=0, keepdims=True)
    ridx = lax.broadcasted_iota(jnp.int32, pe.shape, 0)
    m1 = jnp.max(pe, axis=0, keepdims=True)
    i1 = jnp.min(jnp.where(pe == m1, ridx, EXPERTS_PER_GROUP), axis=0, keepdims=True)
    pe2 = jnp.where(ridx == i1, -1.0, pe)
    m2 = jnp.max(pe2, axis=0, keepdims=True)
    i2 = jnp.min(jnp.where(pe2 == m2, ridx, EXPERTS_PER_GROUP), axis=0, keepdims=True)
    tot = m1 + m2
    w0 = p_grp * (m1 / tot)
    w1 = p_grp * (m2 / tot)
    e0 = g_idx * EXPERTS_PER_GROUP + i1
    e1 = g_idx * EXPERTS_PER_GROUP + i2
    eidx = lax.broadcasted_iota(jnp.int32, (N_EXPERTS, tile), 0)
    oh0 = jnp.where(eidx == e0, 1.0, 0.0)
    oh1 = jnp.where(eidx == e1, 1.0, 0.0)
    before = jnp.where(lax.broadcasted_iota(jnp.int32, (tile, tile), 0)
                       < lax.broadcasted_iota(jnp.int32, (tile, tile), 1), 1.0, 0.0).astype(BF16)
    c0 = jnp.dot(oh0.astype(BF16), before, preferred_element_type=F32)
    c1 = jnp.dot(oh1.astype(BF16), before, preferred_element_type=F32)
    cnt0 = jnp.sum(oh0, axis=1, keepdims=True)
    cnt = cnt0 + jnp.sum(oh1, axis=1, keepdims=True)
    rank0 = jnp.sum(oh0 * c0, axis=0, keepdims=True)
    rank1 = jnp.sum(oh1 * (c1 + cnt0), axis=0, keepdims=True)
    units = jnp.floor((cnt + (MOE_ALIGN - 1)) * (1.0 / MOE_ALIGN))
    lower = jnp.where(lax.broadcasted_iota(jnp.int32, (N_EXPERTS, N_EXPERTS), 1)
                      < lax.broadcasted_iota(jnp.int32, (N_EXPERTS, N_EXPERTS), 0), 1.0, 0.0).astype(BF16)
    loc = jnp.dot(lower, jnp.broadcast_to(units, (N_EXPERTS, tile)).astype(BF16),
                  preferred_element_type=F32) * MOE_ALIGN
    pos0 = jnp.sum(oh0 * loc, axis=0, keepdims=True) + rank0
    pos1 = jnp.sum(oh1 * loc, axis=0, keepdims=True) + rank1
    fields = [pos0, pos1, w0, w1, e0.astype(F32), e1.astype(F32), rank0, rank1]
    frow = lax.broadcasted_iota(jnp.int32, (len(fields), tile), 0)
    info = jnp.zeros((len(fields), tile), F32)
    for k, field in enumerate(fields):
        info = jnp.where(frow == k, field, info)
    info_ref[0] = info
    cnt_ref[0] = jnp.broadcast_to(cnt, (N_EXPERTS, 128))


def _route(logits_t):
    n_tiles = logits_t.shape[0]
    return pl.pallas_call(
        _route_kernel,
        out_shape=[jax.ShapeDtypeStruct((n_tiles, 8, ROW_TILE), F32),
                   jax.ShapeDtypeStruct((n_tiles, N_EXPERTS, 128), F32)],
        grid=(n_tiles,),
        in_specs=[pl.BlockSpec((1, ROUTER_ROWS, ROW_TILE), lambda t: (t, 0, 0))],
        out_specs=[pl.BlockSpec((1, 8, ROW_TILE), lambda t: (t, 0, 0)),
                   pl.BlockSpec((1, N_EXPERTS, 128), lambda t: (t, 0, 0))],
        compiler_params=_cparams("arbitrary"),
        name="moe_route",
    )(logits_t)


def _dispatch_kernel(dst_ref, loc_ref, nch_ref, zfill_ref, info_ref, h2_ref, out_ref,
                     xp_ref, zero_ref, sem, zsem):
    t = pl.program_id(0)
    nt = pl.num_programs(0)
    slot = t % 2
    info = info_ref[0]
    p = lax.broadcasted_iota(jnp.int32, (MOE_TILE_SLOTS, 1), 0).astype(F32)
    sel = jnp.where(p == info[0:1], 1.0, jnp.where(p == info[1:2], 1.0, 0.0)).astype(BF16)
    xp_ref[slot] = jnp.dot(sel, h2_ref[...].astype(BF16), preferred_element_type=F32)

    def copies(tt, sl, start):
        def per_expert(e, carry):
            base = tt * N_EXPERTS + e
            loc = loc_ref[base]
            dst = dst_ref[base]

            def per_chunk(j, carry2):
                cp = pltpu.make_async_copy(
                    xp_ref.at[sl, pl.ds(pl.multiple_of(loc + j * MOE_CAP, MOE_ALIGN), MOE_CAP)],
                    out_ref.at[pl.ds(pl.multiple_of(dst + j * MOE_CAP, MOE_ALIGN), MOE_CAP)],
                    sem.at[sl])
                if start:
                    cp.start()
                else:
                    cp.wait()
                return carry2

            return lax.fori_loop(0, nch_ref[base], per_chunk, carry)

        lax.fori_loop(0, N_EXPERTS, per_expert, 0)

    @pl.when(t > 0)
    def _():
        copies(t - 1, 1 - slot, False)

    @pl.when(t == 0)
    def _():
        zero_ref[...] = jnp.zeros_like(zero_ref)

        def zero_blocks(start):
            def per_block(j, carry):
                @pl.when(zfill_ref[j] == 1)
                def _():
                    cp = pltpu.make_async_copy(
                        zero_ref, out_ref.at[pl.ds(pl.multiple_of(j * MOE_BLOCK, MOE_BLOCK), MOE_BLOCK)], zsem)
                    if start:
                        cp.start()
                    else:
                        cp.wait()

                return carry

            lax.fori_loop(0, out_ref.shape[0] // MOE_BLOCK, per_block, 0)

        zero_blocks(True)
        zero_blocks(False)

    copies(t, slot, True)

    @pl.when(t == nt - 1)
    def _():
        copies(t, slot, False)


def _dispatch(tables, info, h2_flat, n_rows_static):
    n_tiles = info.shape[0]
    grid_spec = pltpu.PrefetchScalarGridSpec(
        num_scalar_prefetch=4,
        grid=(n_tiles,),
        in_specs=[pl.BlockSpec((1, 8, ROW_TILE), lambda t, *_: (t, 0, 0)),
                  pl.BlockSpec((ROW_TILE, D_MODEL), lambda t, *_: (t, 0))],
        out_specs=pl.BlockSpec(memory_space=pl.ANY),
        scratch_shapes=[pltpu.VMEM((2, MOE_TILE_SLOTS, D_MODEL), F32),
                        pltpu.VMEM((MOE_BLOCK, D_MODEL), F32),
                        pltpu.SemaphoreType.DMA((2,)),
                        pltpu.SemaphoreType.DMA(())],
    )
    return pl.pallas_call(
        _dispatch_kernel,
        out_shape=jax.ShapeDtypeStruct((n_rows_static, D_MODEL), F32),
        grid_spec=grid_spec,
        compiler_params=_cparams("arbitrary"),
        name="moe_dispatch",
    )(tables["dst"], tables["loc"], tables["nch"], tables["zfill"], info, h2_flat)


def _ffn_kernel(bexp_ref, used_ref, xblk_ref, first_ref, x_ref, wg_ref, wu_ref, wd_ref, o_ref,
                wg_bf, wu_bf, wd_bf):
    del bexp_ref, xblk_ref
    j = pl.program_id(0)

    @pl.when(first_ref[j] == 1)
    def _():
        wg_bf[...] = wg_ref[0, 0].astype(BF16)
        wu_bf[...] = wu_ref[0, 0].astype(BF16)
        wd_bf[...] = wd_ref[0, 0].astype(BF16)

    @pl.when(used_ref[j] == 1)
    def _():
        x = x_ref[...].astype(BF16)
        gate = jnp.dot(x, wg_bf[...], preferred_element_type=F32)
        up = jnp.dot(x, wu_bf[...], preferred_element_type=F32)
        hid = (gate * jax.nn.sigmoid(gate) * up).astype(BF16)
        o_ref[...] = jnp.dot(hid, wd_bf[...], preferred_element_type=F32)

    @pl.when(used_ref[j] == 0)
    def _():
        o_ref[...] = jnp.zeros_like(o_ref)


def _expert_ffn(tables, xs_sorted, w_gate, w_up, w_down, layer):
    P = xs_sorted.shape[0]
    w_in_spec = pl.BlockSpec((1, 1, D_MODEL, EXPERT_HIDDEN), lambda j, be, *_: (layer, be[j], 0, 0))
    grid_spec = pltpu.PrefetchScalarGridSpec(
        num_scalar_prefetch=4,
        grid=(P // MOE_BLOCK,),
        in_specs=[
            pl.BlockSpec((MOE_BLOCK, D_MODEL), lambda j, be, us, xb, fi: (xb[j], 0)),
            w_in_spec, w_in_spec,
            pl.BlockSpec((1, 1, EXPERT_HIDDEN, D_MODEL), lambda j, be, *_: (layer, be[j], 0, 0)),
        ],
        out_specs=pl.BlockSpec((MOE_BLOCK, D_MODEL), lambda j, *_: (j, 0)),
        scratch_shapes=[pltpu.VMEM((D_MODEL, EXPERT_HIDDEN), BF16),
                        pltpu.VMEM((D_MODEL, EXPERT_HIDDEN), BF16),
                        pltpu.VMEM((EXPERT_HIDDEN, D_MODEL), BF16)],
    )
    return pl.pallas_call(
        _ffn_kernel,
        out_shape=jax.ShapeDtypeStruct((P, D_MODEL), F32),
        grid_spec=grid_spec,
        compiler_params=_cparams("arbitrary"),
        name="expert_ffn",
    )(tables["bexp"], tables["used"], tables["xblk"], tables["first"], xs_sorted, w_gate, w_up, w_down)


def _combine_kernel(dst_ref, nch_ref, nround_ref, infoc_ref, x_ref, mod_ref, ys_ref, o_ref,
                    yp_ref, acc_ref, sem):
    nt = pl.num_programs(1)
    tile = pl.program_id(0) * nt + pl.program_id(1)
    n_tiles = pl.num_programs(0) * nt
    slot = tile % 2

    def stage(tt, sl, rnd, start):
        def per_expert(e, carry):
            base = tt * N_EXPERTS + e

            @pl.when(nch_ref[base] > rnd)
            def _():
                cp = pltpu.make_async_copy(
                    ys_ref.at[pl.ds(pl.multiple_of(dst_ref[base] + rnd * MOE_CAP, MOE_ALIGN), MOE_CAP)],
                    yp_ref.at[sl, pl.ds(pl.multiple_of(e * MOE_CAP, MOE_ALIGN), MOE_CAP)],
                    sem.at[sl])
                if start:
                    cp.start()
                else:
                    cp.wait()

            return carry

        lax.fori_loop(0, N_EXPERTS, per_expert, 0)

    @pl.when(tile == 0)
    def _():
        yp_ref[...] = jnp.zeros_like(yp_ref)
        stage(0, 0, 0, True)

    @pl.when(tile + 1 < n_tiles)
    def _():
        stage(tile + 1, 1 - slot, 0, True)

    stage(tile, slot, 0, False)

    infoc = infoc_ref[0]
    w0, w1 = infoc[:, 2:3], infoc[:, 3:4]
    e0, e1 = infoc[:, 4:5], infoc[:, 5:6]
    r0, r1 = infoc[:, 6:7], infoc[:, 7:8]
    lane = lax.broadcasted_iota(jnp.int32, (1, N_EXPERTS * MOE_CAP), 1).astype(F32)

    def gathered(rnd):
        yp = yp_ref[slot].astype(BF16)

        def target(e, r):
            rr = r - rnd * MOE_CAP
            return jnp.where((rr >= 0.0) & (rr < MOE_CAP), e * MOE_CAP + rr, -1.0)

        tgt = jnp.concatenate([target(e0, r0), target(e1, r1)], axis=0)
        sel = jnp.where(lane == tgt, 1.0, 0.0).astype(BF16)
        picked = jnp.dot(sel, yp, preferred_element_type=F32)
        return w0 * picked[:ROW_TILE] + w1 * picked[ROW_TILE:]

    acc_ref[...] = gathered(0)

    def extra_round(rnd, carry):
        stage(tile, slot, rnd, True)
        stage(tile, slot, rnd, False)
        acc_ref[...] += gathered(rnd.astype(F32))
        return carry

    lax.fori_loop(1, nround_ref[tile], extra_round, 0)
    gate2 = mod_ref[0][:, 5 * D_MODEL:6 * D_MODEL]
    o_ref[0] = x_ref[0] + gate2 * acc_ref[...]


def _combine(tables, info_cols, xs, mod, ys):
    B, N, _ = xs.shape
    nt = N // ROW_TILE
    grid_spec = pltpu.PrefetchScalarGridSpec(
        num_scalar_prefetch=3,
        grid=(B, nt),
        in_specs=[pl.BlockSpec((1, ROW_TILE, 8), lambda b, i, *_: (b * nt + i, 0, 0)),
                  pl.BlockSpec((1, ROW_TILE, D_MODEL), lambda b, i, *_: (b, i, 0)),
                  pl.BlockSpec((1, 1, 6 * D_MODEL), lambda b, i, *_: (jnp.where(i == 0, B, b), 0, 0)),
                  pl.BlockSpec(memory_space=pl.ANY)],
        out_specs=pl.BlockSpec((1, ROW_TILE, D_MODEL), lambda b, i, *_: (b, i, 0)),
        scratch_shapes=[pltpu.VMEM((2, N_EXPERTS * MOE_CAP, D_MODEL), F32),
                        pltpu.VMEM((ROW_TILE, D_MODEL), F32),
                        pltpu.SemaphoreType.DMA((2,))],
    )
    return pl.pallas_call(
        _combine_kernel,
        out_shape=jax.ShapeDtypeStruct((B, N, D_MODEL), F32),
        grid_spec=grid_spec,
        compiler_params=_cparams("arbitrary", "arbitrary"),
        name="moe_combine",
    )(tables["dst"], tables["nch"], tables["nround"], info_cols, xs, mod, ys)


def _moe_static_rows(n_tokens):
    n_tiles = n_tokens // ROW_TILE
    worst = n_tokens * TOP_K + n_tiles * N_EXPERTS * (MOE_ALIGN - 1) + N_EXPERTS * (2 * MOE_BLOCK - 1)
    return -(-worst // MOE_BLOCK) * MOE_BLOCK


def _moe_tables(counts, n_rows_static):
    n_tiles = counts.shape[0]
    cnt = counts.astype(jnp.int32)
    cnt_al = -(-cnt // MOE_ALIGN) * MOE_ALIGN
    seg_off = jnp.cumsum(cnt_al, axis=0) - cnt_al
    tot = jnp.sum(cnt_al, axis=0)
    nblk = -(-tot // MOE_BLOCK)
    esize = (nblk + 1) * MOE_BLOCK
    eend = jnp.cumsum(esize)
    estart = eend - esize
    loc = jnp.cumsum(cnt_al, axis=1) - cnt_al
    nch = -(-cnt_al // MOE_CAP)
    blk0 = jnp.arange(n_rows_static // MOE_BLOCK, dtype=jnp.int32) * MOE_BLOCK
    bexp = jnp.minimum(jnp.sum(blk0[:, None] >= eend[None, :], axis=1), N_EXPERTS - 1).astype(jnp.int32)
    in_expert = blk0 - jnp.sum(jnp.where(blk0[:, None] >= eend[None, :], esize[None, :], 0), axis=1)
    n_used = jnp.sum(jnp.where(jnp.arange(N_EXPERTS)[None, :] == bexp[:, None], nblk[None, :], 0), axis=1)
    used = (in_expert < n_used * MOE_BLOCK) & (blk0 < eend[-1])
    blk_id = jnp.arange(blk0.shape[0], dtype=jnp.int32)
    xblk = jnp.where(used, blk_id, jnp.maximum(blk_id - (in_expert // MOE_BLOCK - n_used) - 1, 0))
    xblk = jnp.where(blk0 < eend[-1], xblk, 0)
    i32 = lambda a: a.astype(jnp.int32)
    return {
        "dst": i32((estart[None, :] + seg_off).reshape(-1)),
        "loc": i32(loc.reshape(-1)),
        "nch": i32(nch.reshape(-1)),
        "nround": i32(jnp.maximum(jnp.max(nch, axis=1), 1)),
        "zfill": i32(~used | (in_expert == (n_used - 1) * MOE_BLOCK)),
        "first": i32(used & (in_expert == 0)),
        "bexp": bexp, "used": i32(used), "xblk": i32(xblk),
    }


def _rope_tables(n_ctx, n_lat):
    row = jnp.repeat(jnp.arange(n_lat // GRID_W, dtype=F32), GRID_W)
    col = jnp.tile(jnp.arange(GRID_W, dtype=F32), n_lat // GRID_W)
    inv = ROPE_BASE ** (-jnp.arange(0, ROPE_AXIS_DIM, 2, dtype=F32) / ROPE_AXIS_DIM)
    ang = jnp.concatenate([row[:, None] * inv, col[:, None] * inv], axis=-1)
    cos = jnp.concatenate([jnp.ones((n_ctx, ROPE_AXIS_DIM), F32), jnp.cos(ang)], axis=0)
    sin = jnp.concatenate([jnp.zeros((n_ctx, ROPE_AXIS_DIM), F32), jnp.sin(ang)], axis=0)
    return jnp.tile(cos, (1, 4)), jnp.tile(jnp.concatenate([-sin, sin], axis=1), (1, 2))


def _block_diag(w):
    g, n, _ = w.shape
    out = jnp.zeros((g * n, g * n), w.dtype)
    for i in range(g):
        out = out.at[i * n:(i + 1) * n, i * n:(i + 1) * n].set(w[i])
    return out


def kernel(x, c, ctx, c_ctx, norm1_g, norm2_g, ada_w, ada_b, w_in, w_out, pool_w, pool_scale,
           q_norm_g, k_norm_g, attn_sink, hgrn_lb, hgrn_norm_g, router_group_w, router_group_b,
           router_expert_w, router_expert_b, expert_w_gate, expert_w_up, expert_w_down):
    B, L, _ = x.shape
    C = ctx.shape[1]
    N = C + L
    depth = ada_w.shape[0]
    assert C == ROW_TILE and L % ROW_TILE == 0 and L % GRID_W == 0

    cos2, sin2 = _rope_tables(C, L)
    sm = jax.nn.softmax(hgrn_lb.astype(F32), axis=0)
    lb_all = jnp.cumsum(sm, axis=0) - sm[0]

    craw = jnp.concatenate([c, c_ctx[None], jnp.zeros((8 - B - 1, D_MODEL), F32)], axis=0)
    mod_all = _ada(craw, ada_w, ada_b)[:, :B + 1].reshape(depth, B + 1, 1, 6 * D_MODEL)

    xs = jnp.concatenate([ctx, x], axis=1)
    for l in range(depth):
        mod = mod_all[l]
        u, q, k, v, hq, hi, hzf, hzb, hgate = _in_proj(xs, norm1_g[l], mod, w_in[l].astype(BF16))
        a = _pool(u, _block_diag(pool_w[l]).astype(BF16), pool_scale[l], C)
        qh, kh, vh = _qk_prep(q, k, v, cos2, sin2,
                              jnp.tile(q_norm_g[l].reshape(1, HEAD_DIM), (1, 2)),
                              jnp.tile(k_norm_g[l].reshape(1, HEAD_DIM), (1, 2)))
        b_mix = _attention(qh, kh, vh, attn_sink[l], C)
        o_f = _hgrn_scan(hq, hi, hzf, lb_all[l, 0], reverse=False)
        c_mix = _hgrn_scan(hq, hi, hzb, lb_all[l, 1], reverse=True, o_fwd=o_f, hgate=hgate,
                           norm_g=hgrn_norm_g[l])
        pad_cols = lambda n: jnp.zeros((D_MODEL, n), F32)
        r_w = jnp.concatenate([router_group_w[l], pad_cols(ROUTER_EXPERT_ROW0 - N_GROUPS),
                               router_expert_w[l],
                               pad_cols(ROUTER_ROWS - ROUTER_EXPERT_ROW0 - N_EXPERTS)], axis=1)
        r_b = jnp.concatenate([router_group_b[l], jnp.zeros((ROUTER_EXPERT_ROW0 - N_GROUPS,), F32),
                               router_expert_b[l],
                               jnp.zeros((ROUTER_ROWS - ROUTER_EXPERT_ROW0 - N_EXPERTS,), F32)])
        r_hi = r_w.astype(BF16)
        r_lo = (r_w - r_hi.astype(F32)).astype(BF16)
        xs, h2, logits_t = _out_proj(a, b_mix, c_mix, xs, w_out[l].astype(BF16), mod, norm2_g[l],
                                     r_hi, r_lo, r_b.reshape(1, ROUTER_ROWS))
        info, counts = _route(logits_t)
        n_rows_static = _moe_static_rows(B * N)
        tables = _moe_tables(counts[:, :, 0], n_rows_static)
        xs_sorted = _dispatch(tables, info, h2.reshape(B * N, D_MODEL), n_rows_static)
        ys = _expert_ffn(tables, xs_sorted, expert_w_gate, expert_w_up, expert_w_down, l)
        xs = _combine(tables, jnp.swapaxes(info, 1, 2), xs, mod, ys)
    return xs[:, C:]
```

```python
import functools

import numpy as np
import jax
import jax.numpy as jnp
from jax import lax
from jax.experimental import pallas as pl
from jax.experimental.pallas import tpu as pltpu

D_MODEL = 1024
DEPTH = 4
GRID_W = 64
NORM_EPS = 1e-6
NEG_INF = -1e30

POOL_WINDOWS = (2, 4, 8, 16)
POOL_GROUP_DIM = 64
POOL_DIM = 256

HEAD_DIM = 64
ATTN_HEADS = 8
ATTN_KV_HEADS = 2
Q_PER_KV = 4
ATTN_DIM = 512
KV_DIM = 128
ATTN_WINDOW = 128
ATTN_BLOCK = 128
ROPE_BASE = 10000.0
ROPE_AXIS_DIM = 32

HG_HEADS = 4
HG_DIM = 64
HG_WIDTH = 256
HG_CHUNK = 64
HG_LEVELS = 7
HG_BATCH_STEP = 2

MIX_WIDTH = 1024
IN_SIZES = (POOL_DIM, ATTN_DIM, KV_DIM, KV_DIM, HG_WIDTH, HG_WIDTH, HG_WIDTH, HG_WIDTH, HG_WIDTH)
IN_OFFS = tuple(int(sum(IN_SIZES[:i])) for i in range(len(IN_SIZES) + 1))
IN_COLS = IN_OFFS[-1]

N_GROUPS = 4
EXPERTS_PER_GROUP = 8
N_EXPERTS = 32
TOP_K = 2
EXPERT_HIDDEN = 512
MOE_BLOCK = 256
ROUTER_ROWS = 128
ROUTER_EXPERT_ROW0 = 8
MOE_ALIGN = 8
MOE_CAP = 32

ROW_TILE = 256
MOE_TILE_SLOTS = TOP_K * ROW_TILE + N_EXPERTS * MOE_ALIGN
ADA_COL_TILE = 1536
VMEM_LIMIT = 56 * 1024 * 1024

F32 = jnp.float32
BF16 = jnp.bfloat16


def _cparams(*sem):
    return pltpu.CompilerParams(dimension_semantics=sem, vmem_limit_bytes=VMEM_LIMIT)


def _rms_scale(x):
    return lax.rsqrt(jnp.mean(x * x, axis=-1, keepdims=True) + NORM_EPS)


def _pack_pairs(x):
    n = x.shape[1] // 2
    lo = lax.bitcast_convert_type(x[:, :n], jnp.uint32) >> 16
    hi = lax.bitcast_convert_type(x[:, n:], jnp.uint32) & jnp.uint32(0xFFFF0000)
    return lo | hi


def _unpack_pairs(u):
    lo = lax.bitcast_convert_type(u << 16, F32)
    hi = lax.bitcast_convert_type(u & jnp.uint32(0xFFFF0000), F32)
    return jnp.concatenate([lo, hi], axis=1)


def _ada_kernel(c_ref, w_ref, b_ref, o_ref):
    s = c_ref[...]
    s = (s * jax.nn.sigmoid(s)).astype(BF16)
    o_ref[0] = jnp.dot(s, w_ref[0].astype(BF16), preferred_element_type=F32) + b_ref[0]


def _ada(craw, ada_w, ada_b):
    depth = ada_w.shape[0]
    ncol = ada_w.shape[2]
    return pl.pallas_call(
        _ada_kernel,
        out_shape=jax.ShapeDtypeStruct((depth, 8, ncol), F32),
        grid=(depth, ncol // ADA_COL_TILE),
        in_specs=[
            pl.BlockSpec((8, D_MODEL), lambda l, j: (0, 0)),
            pl.BlockSpec((1, D_MODEL, ADA_COL_TILE), lambda l, j: (l, 0, j)),
            pl.BlockSpec((1, 1, ADA_COL_TILE), lambda l, j: (l, 0, j)),
        ],
        out_specs=pl.BlockSpec((1, 8, ADA_COL_TILE), lambda l, j: (l, 0, j)),
        compiler_params=_cparams("arbitrary", "arbitrary"),
        name="ada_mod",
    )(craw, ada_w, ada_b.reshape(depth, 1, ncol))


def _in_proj_kernel(x_ref, g_ref, mod_ref, w_ref, *out_refs):
    x = x_ref[0]
    mod = mod_ref[0]
    sh = mod[:, 0:D_MODEL]
    sc = mod[:, D_MODEL:2 * D_MODEL]
    h = (x * _rms_scale(x) * g_ref[...] * (1.0 + sc) + sh).astype(BF16)
    for o_ref, lo, hi in zip(out_refs, IN_OFFS[:-1], IN_OFFS[1:]):
        o_ref[0] = jnp.dot(h, w_ref[:, lo:hi], preferred_element_type=F32)


def _mod_spec(n_batch):
    return pl.BlockSpec((1, 1, 6 * D_MODEL), lambda b, i: (jnp.where(i == 0, n_batch, b), 0, 0))


def _in_proj(xs, g1, mod, w_in_bf16):
    B, N, _ = xs.shape
    nt = N // ROW_TILE
    return pl.pallas_call(
        _in_proj_kernel,
        out_shape=[jax.ShapeDtypeStruct((B, N, w), F32) for w in IN_SIZES],
        grid=(B, nt),
        in_specs=[
            pl.BlockSpec((1, ROW_TILE, D_MODEL), lambda b, i: (b, i, 0)),
            pl.BlockSpec((1, D_MODEL), lambda b, i: (0, 0)),
            _mod_spec(B),
            pl.BlockSpec((D_MODEL, IN_COLS), lambda b, i: (0, 0)),
        ],
        out_specs=[pl.BlockSpec((1, ROW_TILE, w), lambda b, i: (b, i, 0)) for w in IN_SIZES],
        compiler_params=_cparams("arbitrary", "arbitrary"),
        name="in_proj",
    )(xs, g1.reshape(1, D_MODEL), mod, w_in_bf16)


def _qk_prep_kernel(q_ref, k_ref, v_ref, cos_ref, sin_ref, gq_ref, gk_ref, qh_ref, kh_ref, vh_ref):
    cos = cos_ref[...]
    sin = sin_ref[...]
    lane = lax.broadcasted_iota(jnp.int32, (1, 2 * HEAD_DIM), 1)
    lo_head = lane < HEAD_DIM
    first_half = (lane % HEAD_DIM) < (HEAD_DIM // 2)

    def norm_rope(x2, g, scale):
        sq = x2 * x2
        s0 = jnp.sum(jnp.where(lo_head, sq, 0.0), axis=-1, keepdims=True)
        s1 = jnp.sum(jnp.where(lo_head, 0.0, sq), axis=-1, keepdims=True)
        ms = jnp.where(lo_head, s0, s1) * (1.0 / HEAD_DIM)
        xn = x2 * lax.rsqrt(ms + NORM_EPS) * g
        swapped = jnp.where(first_half, pltpu.roll(xn, 2 * HEAD_DIM - HEAD_DIM // 2, 1),
                            pltpu.roll(xn, HEAD_DIM // 2, 1))
        return (xn * cos + swapped * sin) * scale

    for p in range(ATTN_HEADS // 2):
        y = norm_rope(q_ref[0, :, 2 * HEAD_DIM * p:2 * HEAD_DIM * (p + 1)], gq_ref[...], HEAD_DIM ** -0.5)
        qh_ref[0, 2 * p] = y[:, :HEAD_DIM].astype(BF16)
        qh_ref[0, 2 * p + 1] = y[:, HEAD_DIM:].astype(BF16)
    y = norm_rope(k_ref[0], gk_ref[...], 1.0)
    kh_ref[0, 0] = y[:, :HEAD_DIM].astype(BF16)
    kh_ref[0, 1] = y[:, HEAD_DIM:].astype(BF16)
    v = v_ref[0]
    vh_ref[0, 0] = v[:, :HEAD_DIM].astype(BF16)
    vh_ref[0, 1] = v[:, HEAD_DIM:].astype(BF16)


def _qk_prep(q, k, v, cos2, sin2, gq2, gk2):
    B, N, _ = q.shape
    nt = N // ROW_TILE
    row = lambda w: pl.BlockSpec((1, ROW_TILE, w), lambda b, i: (b, i, 0))
    tab = pl.BlockSpec((ROW_TILE, 2 * HEAD_DIM), lambda b, i: (i, 0))
    vec = pl.BlockSpec((1, 2 * HEAD_DIM), lambda b, i: (0, 0))
    head = lambda n: pl.BlockSpec((1, n, ROW_TILE, HEAD_DIM), lambda b, i: (b, 0, i, 0))
    return pl.pallas_call(
        _qk_prep_kernel,
        out_shape=[jax.ShapeDtypeStruct((B, ATTN_HEADS, N, HEAD_DIM), BF16),
                   jax.ShapeDtypeStruct((B, ATTN_KV_HEADS, N, HEAD_DIM), BF16),
                   jax.ShapeDtypeStruct((B, ATTN_KV_HEADS, N, HEAD_DIM), BF16)],
        grid=(B, nt),
        in_specs=[row(ATTN_DIM), row(KV_DIM), row(KV_DIM), tab, tab, vec, vec],
        out_specs=[head(ATTN_HEADS), head(ATTN_KV_HEADS), head(ATTN_KV_HEADS)],
        compiler_params=_cparams("arbitrary", "arbitrary"),
        name="qk_prep",
    )(q, k, v, cos2, sin2, gq2, gk2)


def _attn_kernel(sink_ref, q_ref, kp_ref, kc_ref, kn_ref, kx_ref, vp_ref, vc_ref, vn_ref, vx_ref,
                 o_ref, *, n_ctx, n_rows):
    i = pl.program_id(1)
    blk = ATTN_BLOCK
    t = i * blk + lax.broadcasted_iota(jnp.int32, (blk, 1), 0)
    kr = (i - 1) * blk + lax.broadcasted_iota(jnp.int32, (1, 3 * blk), 1)
    ok_loc = (t >= n_ctx) & (kr >= n_ctx) & (kr < n_rows) & (jnp.abs(t - kr) <= ATTN_WINDOW)
    bias = jnp.concatenate([jnp.where(ok_loc, 0.0, NEG_INF), jnp.zeros((blk, n_ctx), F32)],
                           axis=1)[None]
    heads = []
    for g in range(ATTN_KV_HEADS):
        q = q_ref[0, Q_PER_KV * g:Q_PER_KV * (g + 1)].reshape(Q_PER_KV * blk, HEAD_DIM)
        k = jnp.concatenate([kp_ref[0, g], kc_ref[0, g], kn_ref[0, g], kx_ref[0, g]], axis=0)
        v = jnp.concatenate([vp_ref[0, g], vc_ref[0, g], vn_ref[0, g], vx_ref[0, g]], axis=0)
        s = lax.dot_general(q, k, (((1,), (1,)), ((), ())), preferred_element_type=F32)
        nk = s.shape[-1]
        s = (s.reshape(Q_PER_KV, blk, nk) + bias).reshape(Q_PER_KV * blk, nk)
        sk = jnp.concatenate(
            [jnp.full((blk, 1), sink_ref[Q_PER_KV * g + hh], F32) for hh in range(Q_PER_KV)], axis=0)
        m = jnp.maximum(jnp.max(s, axis=-1, keepdims=True), sk)
        p = jnp.exp(s - m)
        denom = jnp.sum(p, axis=-1, keepdims=True) + jnp.exp(sk - m)
        o = jnp.dot(p.astype(BF16), v, preferred_element_type=F32) / denom
        heads += [o[hh * blk:(hh + 1) * blk] for hh in range(Q_PER_KV)]
    o_ref[0] = jnp.concatenate(heads, axis=1).astype(BF16)


def _attention(qh, kh, vh, sink, n_ctx):
    B, _, N, _ = qh.shape
    nq = N // ATTN_BLOCK
    kv = lambda f: pl.BlockSpec((1, ATTN_KV_HEADS, ATTN_BLOCK, HEAD_DIM), f)
    prev = lambda b, i: (b, 0, jnp.maximum(i - 1, 0), 0)
    cur = lambda b, i: (b, 0, i, 0)
    nxt = lambda b, i: (b, 0, jnp.minimum(i + 1, nq - 1), 0)
    ctx = pl.BlockSpec((1, ATTN_KV_HEADS, n_ctx, HEAD_DIM), lambda b, i: (b, 0, 0, 0))
    return pl.pallas_call(
        functools.partial(_attn_kernel, n_ctx=n_ctx, n_rows=N),
        out_shape=jax.ShapeDtypeStruct((B, N, ATTN_DIM), BF16),
        grid=(B, nq),
        in_specs=[pl.BlockSpec(memory_space=pltpu.SMEM),
                  pl.BlockSpec((1, ATTN_HEADS, ATTN_BLOCK, HEAD_DIM), cur),
                  kv(prev), kv(cur), kv(nxt), ctx, kv(prev), kv(cur), kv(nxt), ctx],
        out_specs=pl.BlockSpec((1, ATTN_BLOCK, ATTN_DIM), lambda b, i: (b, i, 0)),
        compiler_params=_cparams("arbitrary", "arbitrary"),
        name="band_attention",
    )(sink, qh, kh, kh, kh, kh, vh, vh, vh, vh)


def _hgrn_tables(reverse):
    c = HG_CHUNK
    t = np.arange(c)
    msum = np.zeros((HG_LEVELS + 1, c, c), np.float32)
    qm = np.zeros((HG_LEVELS, c, 1), np.float32)
    km = np.zeros((HG_LEVELS, c, 1), np.float32)
    sm = np.zeros((HG_LEVELS, c, c), np.float32)
    qm[0] = 1.0
    km[0] = 1.0
    sm[0] = np.eye(c)
    for j in range(1, HG_LEVELS):
        m = c >> (j - 1)
        half = m // 2
        later = (t % m) >= half
        mid = (t // m) * m + half
        r = t[None, :]
        rows_later = later[:, None] & (r > mid[:, None]) & (r <= t[:, None])
        rows_early = (~later)[:, None] & (r > t[:, None]) & (r <= mid[:, None])
        msum[j - 1] = (rows_later | rows_early).astype(np.float32)
        qm[j, :, 0] = later
        km[j, :, 0] = ~later
        sm[j] = ((t[:, None] // m) == (t[None, :] // m)).astype(np.float32)
    msum[HG_LEVELS - 1] = (t[None, :] <= t[:, None]).astype(np.float32)
    msum[HG_LEVELS] = (t[None, :] > t[:, None]).astype(np.float32)
    if reverse:
        msum = msum[:, ::-1, ::-1]
        qm = qm[:, ::-1]
        km = km[:, ::-1]
        sm = sm[:, ::-1, ::-1]
    ones = np.ones((1, 1, HG_WIDTH), np.float32)
    return (jnp.asarray(msum.reshape(-1, c), BF16),
            jnp.asarray(qm * ones), jnp.asarray(km * ones),
            jnp.asarray(np.tile(sm, (1, 1, HG_HEADS))))


def _head_mask():
    h = np.arange(HG_WIDTH) // HG_DIM
    return (h[:, None] == h[None, :]).astype(np.float32)


def _hgrn_kernel(*refs, reverse):
    if reverse:
        (hq_ref, hi_ref, hz_ref, lb_ref, ms_ref, qm_ref, km_ref, sm_ref, hm_ref,
         of_ref, hg_ref, hmean_ref, ng_ref, o_ref, st_ref) = refs
    else:
        (hq_ref, hi_ref, hz_ref, lb_ref, ms_ref, qm_ref, km_ref, sm_ref, hm_ref,
         o_ref, st_ref) = refs
    c = HG_CHUNK
    n_chunks = ROW_TILE // c

    @pl.when(pl.program_id(1) == 0)
    def _():
        st_ref[...] = jnp.zeros_like(st_ref)

    lb = lb_ref[...]
    hm_bf = hm_ref[...]
    hm = hm_bf.astype(F32)
    outs = [[None] * n_chunks for _ in range(HG_BATCH_STEP)]
    order = range(n_chunks - 1, -1, -1) if reverse else range(n_chunks)
    for ci, bb in [(ci, bb) for ci in order for bb in range(HG_BATCH_STEP)]:
        rows = slice(ci * c, (ci + 1) * c)
        hq = hq_ref[bb, rows, :]
        q = hq * jax.nn.sigmoid(hq)
        v = hi_ref[bb, rows, :]
        z = hz_ref[bb, rows, :]
        g = jnp.log(lb + (1.0 - lb) * jax.nn.sigmoid(z))
        kk = (1.0 - lb) * jax.nn.sigmoid(-z)
        g_hi = g.astype(BF16)
        g_lo = (g - g_hi.astype(F32)).astype(BF16)
        dsum = jnp.dot(ms_ref[...], jnp.concatenate([g_hi, g_lo], axis=1), preferred_element_type=F32)
        e = jnp.exp(dsum[:, :HG_WIDTH] + dsum[:, HG_WIDTH:])
        scores = jnp.zeros((c, HG_HEADS * c), F32)
        for lv in range(HG_LEVELS):
            if lv == 0:
                ql, kl = q, kk
            else:
                el = e[(lv - 1) * c:lv * c]
                ql = q * el * qm_ref[lv]
                kl = kk * el * km_ref[lv]
            kbd = jnp.concatenate([kl.astype(BF16)] * HG_HEADS, axis=0) * hm_bf
            s_l = lax.dot_general(ql.astype(BF16), kbd, (((1,), (1,)), ((), ())),
                                  preferred_element_type=F32)
            scores = scores + s_l * sm_ref[lv]
        vbd = jnp.concatenate([v.astype(BF16)] * HG_HEADS, axis=0) * hm_bf
        o = jnp.dot(scores.astype(BF16), vbd, preferred_element_type=F32)
        e_cum = e[(HG_LEVELS - 1) * c:HG_LEVELS * c]
        st = st_ref[bb]
        o = o + lax.dot_general((q * e_cum).astype(BF16), st.astype(BF16), (((1,), (1,)), ((), ())),
                                preferred_element_type=F32)
        k_dec = kk * e[HG_LEVELS * c:(HG_LEVELS + 1) * c]
        e_end = e_cum[0:1] if reverse else e_cum[c - 1:c]
        upd = jnp.dot(v.T.astype(BF16), k_dec.astype(BF16), preferred_element_type=F32)
        st_ref[bb] = st * e_end + upd * hm
        outs[bb][ci] = o
    for bb in range(HG_BATCH_STEP):
        o_all = jnp.concatenate(outs[bb], axis=0)
        if reverse:
            o_sum = of_ref[bb] + o_all
            ms = jnp.dot((o_sum * o_sum).astype(BF16), hmean_ref[...], preferred_element_type=F32)
            gate = hg_ref[bb]
            o_ref[bb] = (o_sum * lax.rsqrt(ms + NORM_EPS) * ng_ref[...]
                         * (gate * jax.nn.sigmoid(gate))).astype(BF16)
        else:
            o_ref[bb] = o_all


def _hgrn_scan(hq, hi, hz, lb, reverse, o_fwd=None, hgate=None, norm_g=None):
    B, N, _ = hq.shape
    nt = N // ROW_TILE
    if reverse:
        order = lambda b, i: (b, jnp.where(i == 0, 0, nt - i), 0)
    else:
        order = lambda b, i: (b, i, 0)
    assert B % HG_BATCH_STEP == 0
    row = pl.BlockSpec((HG_BATCH_STEP, ROW_TILE, HG_WIDTH), order)
    const = lambda a: pl.BlockSpec(a.shape, lambda b, i: (0,) * a.ndim)
    msum, qm, km, sm = _hgrn_tables(reverse)
    hm = jnp.asarray(_head_mask(), BF16)
    args = [hq, hi, hz, lb.reshape(1, HG_WIDTH), msum, qm, km, sm, hm]
    specs = [row, row, row] + [const(a) for a in args[3:]]
    if reverse:
        hmean = jnp.asarray(_head_mask() / HG_DIM, BF16)
        ng = jnp.tile(norm_g.reshape(1, HG_DIM), (1, HG_HEADS))
        args += [o_fwd, hgate, hmean, ng]
        specs += [row, row, const(hmean), const(ng)]
    return pl.pallas_call(
        functools.partial(_hgrn_kernel, reverse=reverse),
        out_shape=jax.ShapeDtypeStruct((B, N, HG_WIDTH), BF16 if reverse else F32),
        grid=(B // HG_BATCH_STEP, nt),
        in_specs=specs,
        out_specs=row,
        scratch_shapes=[pltpu.VMEM((HG_BATCH_STEP, HG_WIDTH, HG_WIDTH), F32)],
        compiler_params=_cparams("arbitrary", "arbitrary"),
        name="hgrn_bwd" if reverse else "hgrn_fwd",
    )(*args)


POOL_HALO = 8


def _pool_kernel(up_ref, u_ref, un_ref, w_ref, sc_ref, o_ref, e_ref, *, n_ctx, n_lat):
    i = pl.program_id(1)
    nt = pl.num_programs(1)
    u = u_ref[0]
    has_prev = i >= 2
    has_next = (i >= 1) & (i <= nt - 2)
    e_ref[0:POOL_HALO, :] = jnp.where(has_prev, up_ref[0], 0.0)
    e_ref[POOL_HALO:POOL_HALO + ROW_TILE, :] = u
    e_ref[POOL_HALO + ROW_TILE:, :] = jnp.where(has_next, un_ref[0], 0.0)
    seg_start = jnp.where(i == 0, 0, n_ctx)
    seg_len = jnp.where(i == 0, n_ctx, n_lat)
    tau = i * ROW_TILE - seg_start + lax.broadcasted_iota(jnp.int32, (ROW_TILE, 1), 0)
    lane_grp = lax.broadcasted_iota(jnp.int32, (1, 2 * POOL_GROUP_DIM), 1) // POOL_GROUP_DIM
    halves = []
    for hb in range(2):
        w_a, w_b = POOL_WINDOWS[2 * hb], POOL_WINDOWS[2 * hb + 1]
        lanes = slice(2 * POOL_GROUP_DIM * hb, 2 * POOL_GROUP_DIM * (hb + 1))
        acc = jnp.zeros((ROW_TILE, 2 * POOL_GROUP_DIM), F32)
        for off in range(-((w_b - 1) // 2), w_b // 2 + 1):
            x = e_ref[POOL_HALO + off:POOL_HALO + off + ROW_TILE, lanes]
            if -((w_a - 1) // 2) <= off <= w_a // 2:
                acc = acc + x
            else:
                acc = acc + jnp.where(lane_grp == 1, x, 0.0)

        def count(w):
            lo = jnp.maximum(tau - (w - 1) // 2, 0)
            hi = jnp.minimum(tau + w // 2 + 1, seg_len)
            return (hi - lo).astype(F32)

        cnt = jnp.where(lane_grp == 0, count(w_a), count(w_b))
        halves.append(acc / cnt)
    pooled = jnp.concatenate(halves, axis=1) - u
    y = jnp.dot(pooled.astype(BF16), w_ref[...], preferred_element_type=F32) * sc_ref[...]
    o_ref[0] = y.astype(BF16)


def _pool(u, w_bd, scale, n_ctx):
    B, N, _ = u.shape
    nt = N // ROW_TILE
    per = ROW_TILE // POOL_HALO
    nh = N // POOL_HALO
    return pl.pallas_call(
        functools.partial(_pool_kernel, n_ctx=n_ctx, n_lat=N - n_ctx),
        out_shape=jax.ShapeDtypeStruct((B, N, POOL_DIM), BF16),
        grid=(B, nt),
        in_specs=[
            pl.BlockSpec((1, POOL_HALO, POOL_DIM), lambda b, i: (b, jnp.maximum(i * per - 1, 0), 0)),
            pl.BlockSpec((1, ROW_TILE, POOL_DIM), lambda b, i: (b, i, 0)),
            pl.BlockSpec((1, POOL_HALO, POOL_DIM), lambda b, i: (b, jnp.minimum((i + 1) * per, nh - 1), 0)),
            pl.BlockSpec((POOL_DIM, POOL_DIM), lambda b, i: (0, 0)),
            pl.BlockSpec((1, POOL_DIM), lambda b, i: (0, 0)),
        ],
        out_specs=pl.BlockSpec((1, ROW_TILE, POOL_DIM), lambda b, i: (b, i, 0)),
        scratch_shapes=[pltpu.VMEM((ROW_TILE + 2 * POOL_HALO, POOL_DIM), F32)],
        compiler_params=_cparams("arbitrary", "arbitrary"),
        name="pool_mixer",
    )(u, u, u, w_bd, scale.reshape(1, POOL_DIM))


def _out_proj_kernel(a_ref, b_ref, c_ref, x_ref, w_ref, mod_ref, g2_ref, rh_ref, rl_ref, rb_ref,
                     xo_ref, h2_ref, lg_ref):
    mix = (jnp.dot(a_ref[0], w_ref[0:POOL_DIM, :], preferred_element_type=F32)
           + jnp.dot(b_ref[0], w_ref[POOL_DIM:POOL_DIM + ATTN_DIM, :], preferred_element_type=F32)
           + jnp.dot(c_ref[0], w_ref[POOL_DIM + ATTN_DIM:, :], preferred_element_type=F32))
    mod = mod_ref[0]
    gate1 = mod[:, 2 * D_MODEL:3 * D_MODEL]
    sh2 = mod[:, 3 * D_MODEL:4 * D_MODEL]
    sc2 = mod[:, 4 * D_MODEL:5 * D_MODEL]
    x = x_ref[0] + gate1 * mix
    xo_ref[0] = x
    h2 = x * _rms_scale(x) * g2_ref[...] * (1.0 + sc2) + sh2
    h2_ref[0] = h2
    h_hi = h2.astype(BF16)
    h_lo = (h2 - h_hi.astype(F32)).astype(BF16)
    lg = (jnp.dot(h_hi, rh_ref[...], preferred_element_type=F32)
          + jnp.dot(h_lo, rh_ref[...], preferred_element_type=F32)
          + jnp.dot(h_hi, rl_ref[...], preferred_element_type=F32) + rb_ref[...])
    lg_ref[0] = lg.T


def _out_proj(a, b, c, xs, w_out_bf16, mod, g2, r_hi, r_lo, r_b):
    B, N, _ = xs.shape
    nt = N // ROW_TILE
    row = lambda w: pl.BlockSpec((1, ROW_TILE, w), lambda b_, i: (b_, i, 0))
    full = lambda s: pl.BlockSpec(s, lambda b_, i: (0,) * len(s))
    return pl.pallas_call(
        _out_proj_kernel,
        out_shape=[jax.ShapeDtypeStruct((B, N, D_MODEL), F32),
                   jax.ShapeDtypeStruct((B, N, D_MODEL), F32),
                   jax.ShapeDtypeStruct((B * nt, ROUTER_ROWS, ROW_TILE), F32)],
        grid=(B, nt),
        in_specs=[row(POOL_DIM), row(ATTN_DIM), row(HG_WIDTH), row(D_MODEL),
                  full((MIX_WIDTH, D_MODEL)), _mod_spec(B), full((1, D_MODEL)),
                  full((D_MODEL, ROUTER_ROWS)), full((D_MODEL, ROUTER_ROWS)), full((1, ROUTER_ROWS))],
        out_specs=[row(D_MODEL), row(D_MODEL),
                   pl.BlockSpec((1, ROUTER_ROWS, ROW_TILE), lambda b_, i: (b_ * nt + i, 0, 0))],
        compiler_params=_cparams("arbitrary", "arbitrary"),
        name="out_proj_router",
    )(a, b, c, xs, w_out_bf16, mod, g2.reshape(1, D_MODEL), r_hi, r_lo, r_b)


def _route_kernel(lg_ref, info_ref, cnt_ref):
    lgt = lg_ref[0]
    tile = lgt.shape[1]
    lg = lgt[0:N_GROUPS]
    gmax = jnp.max(lg, axis=0, keepdims=True)
    ridx = lax.broadcasted_iota(jnp.int32, lg.shape, 0)
    g_idx = jnp.min(jnp.where(lg == gmax, ridx, N_GROUPS), axis=0, keepdims=True)
    p_grp = 1.0 / jnp.sum(jnp.exp(lg - gmax), axis=0, keepdims=True)
    le = jnp.zeros((EXPERTS_PER_GROUP, tile), F32)
    for gi in range(N_GROUPS):
        lo = ROUTER_EXPERT_ROW0 + gi * EXPERTS_PER_GROUP
        le = jnp.where(g_idx == gi, lgt[lo:lo + EXPERTS_PER_GROUP], le)
    ex = jnp.exp(le - jnp.max(le, axis=0, keepdims=True))
    pe = ex / jnp.sum(ex, axis=0, keepdims=True)
    ridx = lax.broadcasted_iota(jnp.int32, pe.shape, 0)
    m1 = jnp.max(pe, axis=0, keepdims=True)
    i1 = jnp.min(jnp.where(pe == m1, ridx, EXPERTS_PER_GROUP), axis=0, keepdims=True)
    pe2 = jnp.where(ridx == i1, -1.0, pe)
    m2 = jnp.max(pe2, axis=0, keepdims=True)
    i2 = jnp.min(jnp.where(pe2 == m2, ridx, EXPERTS_PER_GROUP), axis=0, keepdims=True)
    tot = m1 + m2
    w0 = p_grp * (m1 / tot)
    w1 = p_grp * (m2 / tot)
    e0 = g_idx * EXPERTS_PER_GROUP + i1
    e1 = g_idx * EXPERTS_PER_GROUP + i2
    eidx = lax.broadcasted_iota(jnp.int32, (N_EXPERTS, tile), 0)
    oh0 = jnp.where(eidx == e0, 1.0, 0.0)
    oh1 = jnp.where(eidx == e1, 1.0, 0.0)
    before = jnp.where(lax.broadcasted_iota(jnp.int32, (tile, tile), 0)
                       < lax.broadcasted_iota(jnp.int32, (tile, tile), 1), 1.0, 0.0).astype(BF16)
    c0 = jnp.dot(oh0.astype(BF16), before, preferred_element_type=F32)
    c1 = jnp.dot(oh1.astype(BF16), before, preferred_element_type=F32)
    cnt0 = jnp.sum(oh0, axis=1, keepdims=True)
    cnt = cnt0 + jnp.sum(oh1, axis=1, keepdims=True)
    rank0 = jnp.sum(oh0 * c0, axis=0, keepdims=True)
    rank1 = jnp.sum(oh1 * (c1 + cnt0), axis=0, keepdims=True)
    units = jnp.floor((cnt + (MOE_ALIGN - 1)) * (1.0 / MOE_ALIGN))
    lower = jnp.where(lax.broadcasted_iota(jnp.int32, (N_EXPERTS, N_EXPERTS), 1)
                      < lax.broadcasted_iota(jnp.int32, (N_EXPERTS, N_EXPERTS), 0), 1.0, 0.0).astype(BF16)
    loc = jnp.dot(lower, jnp.broadcast_to(units, (N_EXPERTS, tile)).astype(BF16),
                  preferred_element_type=F32) * MOE_ALIGN
    pos0 = jnp.sum(oh0 * loc, axis=0, keepdims=True) + rank0
    pos1 = jnp.sum(oh1 * loc, axis=0, keepdims=True) + rank1
    fields = [pos0, pos1, w0, w1, e0.astype(F32), e1.astype(F32), rank0, rank1]
    frow = lax.broadcasted_iota(jnp.int32, (len(fields), tile), 0)
    info = jnp.zeros((len(fields), tile), F32)
    for k, field in enumerate(fields):
        info = jnp.where(frow == k, field, info)
    info_ref[0] = info
    cnt_ref[0] = jnp.broadcast_to(cnt, (N_EXPERTS, 128))


def _route(logits_t):
    n_tiles = logits_t.shape[0]
    return pl.pallas_call(
        _route_kernel,
        out_shape=[jax.ShapeDtypeStruct((n_tiles, 8, ROW_TILE), F32),
                   jax.ShapeDtypeStruct((n_tiles, N_EXPERTS, 128), F32)],
        grid=(n_tiles,),
        in_specs=[pl.BlockSpec((1, ROUTER_ROWS, ROW_TILE), lambda t: (t, 0, 0))],
        out_specs=[pl.BlockSpec((1, 8, ROW_TILE), lambda t: (t, 0, 0)),
                   pl.BlockSpec((1, N_EXPERTS, 128), lambda t: (t, 0, 0))],
        compiler_params=_cparams("arbitrary"),
        name="moe_route",
    )(logits_t)


def _dispatch_kernel(dst_ref, loc_ref, nch_ref, zfill_ref, info_ref, h2_ref, out_ref,
                     xp_ref, zero_ref, sem, zsem):
    t = pl.program_id(0)
    nt = pl.num_programs(0)
    slot = t % 2
    info = info_ref[0]
    p = lax.broadcasted_iota(jnp.int32, (MOE_TILE_SLOTS, 1), 0).astype(F32)
    sel = jnp.where(p == info[0:1], 1.0, jnp.where(p == info[1:2], 1.0, 0.0)).astype(BF16)
    xp_ref[slot] = _pack_pairs(jnp.dot(sel, h2_ref[...].astype(BF16), preferred_element_type=F32))

    def copies(tt, sl, start):
        def per_expert(e, carry):
            base = tt * N_EXPERTS + e
            loc = loc_ref[base]
            dst = dst_ref[base]

            def per_chunk(j, carry2):
                cp = pltpu.make_async_copy(
                    xp_ref.at[sl, pl.ds(pl.multiple_of(loc + j * MOE_CAP, MOE_ALIGN), MOE_CAP)],
                    out_ref.at[pl.ds(pl.multiple_of(dst + j * MOE_CAP, MOE_ALIGN), MOE_CAP)],
                    sem.at[sl])
                if start:
                    cp.start()
                else:
                    cp.wait()
                return carry2

            return lax.fori_loop(0, nch_ref[base], per_chunk, carry)

        lax.fori_loop(0, N_EXPERTS, per_expert, 0)

    @pl.when(t > 0)
    def _():
        copies(t - 1, 1 - slot, False)

    @pl.when(t == 0)
    def _():
        zero_ref[...] = jnp.zeros_like(zero_ref)

        def zero_blocks(start):
            def per_block(j, carry):
                @pl.when(zfill_ref[j] == 1)
                def _():
                    cp = pltpu.make_async_copy(
                        zero_ref, out_ref.at[pl.ds(pl.multiple_of(j * MOE_BLOCK, MOE_BLOCK), MOE_BLOCK)], zsem)
                    if start:
                        cp.start()
                    else:
                        cp.wait()

                return carry

            lax.fori_loop(0, out_ref.shape[0] // MOE_BLOCK, per_block, 0)

        zero_blocks(True)
        zero_blocks(False)

    copies(t, slot, True)

    @pl.when(t == nt - 1)
    def _():
        copies(t, slot, False)


def _dispatch(tables, info, h2_flat, n_rows_static):
    n_tiles = info.shape[0]
    grid_spec = pltpu.PrefetchScalarGridSpec(
        num_scalar_prefetch=4,
        grid=(n_tiles,),
        in_specs=[pl.BlockSpec((1, 8, ROW_TILE), lambda t, *_: (t, 0, 0)),
                  pl.BlockSpec((ROW_TILE, D_MODEL), lambda t, *_: (t, 0))],
        out_specs=pl.BlockSpec(memory_space=pl.ANY),
        scratch_shapes=[pltpu.VMEM((2, MOE_TILE_SLOTS, D_MODEL // 2), jnp.uint32),
                        pltpu.VMEM((MOE_BLOCK, D_MODEL // 2), jnp.uint32),
                        pltpu.SemaphoreType.DMA((2,)),
                        pltpu.SemaphoreType.DMA(())],
    )
    return pl.pallas_call(
        _dispatch_kernel,
        out_shape=jax.ShapeDtypeStruct((n_rows_static, D_MODEL // 2), jnp.uint32),
        grid_spec=grid_spec,
        compiler_params=_cparams("arbitrary"),
        name="moe_dispatch",
    )(tables["dst"], tables["loc"], tables["nch"], tables["zfill"], info, h2_flat)


def _ffn_kernel(bexp_ref, used_ref, xblk_ref, first_ref, x_ref, wg_ref, wu_ref, wd_ref, o_ref,
                wg_bf, wu_bf, wd_bf):
    del bexp_ref, xblk_ref
    j = pl.program_id(0)

    @pl.when(first_ref[j] == 1)
    def _():
        wg_bf[...] = wg_ref[0, 0].astype(BF16)
        wu_bf[...] = wu_ref[0, 0].astype(BF16)
        wd_bf[...] = wd_ref[0, 0].astype(BF16)

    @pl.when(used_ref[j] == 1)
    def _():
        x = _unpack_pairs(x_ref[...]).astype(BF16)
        gate = jnp.dot(x, wg_bf[...], preferred_element_type=F32)
        up = jnp.dot(x, wu_bf[...], preferred_element_type=F32)
        hid = (gate * jax.nn.sigmoid(gate) * up).astype(BF16)
        y = jnp.dot(hid, wd_bf[...], preferred_element_type=F32)
        o_ref[...] = _pack_pairs(y.astype(BF16).astype(F32))

    @pl.when(used_ref[j] == 0)
    def _():
        o_ref[...] = jnp.zeros_like(o_ref)


def _expert_ffn(tables, xs_sorted, w_gate, w_up, w_down, layer):
    P = xs_sorted.shape[0]
    w_in_spec = pl.BlockSpec((1, 1, D_MODEL, EXPERT_HIDDEN), lambda j, be, *_: (layer, be[j], 0, 0))
    grid_spec = pltpu.PrefetchScalarGridSpec(
        num_scalar_prefetch=4,
        grid=(P // MOE_BLOCK,),
        in_specs=[
            pl.BlockSpec((MOE_BLOCK, D_MODEL // 2), lambda j, be, us, xb, fi: (xb[j], 0)),
            w_in_spec, w_in_spec,
            pl.BlockSpec((1, 1, EXPERT_HIDDEN, D_MODEL), lambda j, be, *_: (layer, be[j], 0, 0)),
        ],
        out_specs=pl.BlockSpec((MOE_BLOCK, D_MODEL // 2), lambda j, *_: (j, 0)),
        scratch_shapes=[pltpu.VMEM((D_MODEL, EXPERT_HIDDEN), BF16),
                        pltpu.VMEM((D_MODEL, EXPERT_HIDDEN), BF16),
                        pltpu.VMEM((EXPERT_HIDDEN, D_MODEL), BF16)],
    )
    return pl.pallas_call(
        _ffn_kernel,
        out_shape=jax.ShapeDtypeStruct((P, D_MODEL // 2), jnp.uint32),
        grid_spec=grid_spec,
        compiler_params=_cparams("arbitrary"),
        name="expert_ffn",
    )(tables["bexp"], tables["used"], tables["xblk"], tables["first"], xs_sorted, w_gate, w_up, w_down)


def _combine_kernel(dst_ref, nch_ref, nround_ref, infoc_ref, x_ref, mod_ref, ys_ref, o_ref,
                    yp_ref, acc_ref, sem):
    nt = pl.num_programs(1)
    tile = pl.program_id(0) * nt + pl.program_id(1)
    n_tiles = pl.num_programs(0) * nt
    slot = tile % 2

    def stage(tt, sl, rnd, start):
        def per_expert(e, carry):
            base = tt * N_EXPERTS + e

            @pl.when(nch_ref[base] > rnd)
            def _():
                cp = pltpu.make_async_copy(
                    ys_ref.at[pl.ds(pl.multiple_of(dst_ref[base] + rnd * MOE_CAP, MOE_ALIGN), MOE_CAP)],
                    yp_ref.at[sl, pl.ds(pl.multiple_of(e * MOE_CAP, MOE_ALIGN), MOE_CAP)],
                    sem.at[sl])
                if start:
                    cp.start()
                else:
                    cp.wait()

            return carry

        lax.fori_loop(0, N_EXPERTS, per_expert, 0)

    @pl.when(tile == 0)
    def _():
        yp_ref[...] = jnp.zeros_like(yp_ref)
        stage(0, 0, 0, True)

    @pl.when(tile + 1 < n_tiles)
    def _():
        stage(tile + 1, 1 - slot, 0, True)

    stage(tile, slot, 0, False)

    infoc = infoc_ref[0]
    w0, w1 = infoc[:, 2:3], infoc[:, 3:4]
    e0, e1 = infoc[:, 4:5], infoc[:, 5:6]
    r0, r1 = infoc[:, 6:7], infoc[:, 7:8]
    lane = lax.broadcasted_iota(jnp.int32, (1, N_EXPERTS * MOE_CAP), 1).astype(F32)

    def gathered(rnd):
        yp = _unpack_pairs(yp_ref[slot]).astype(BF16)

        def target(e, r):
            rr = r - rnd * MOE_CAP
            return jnp.where((rr >= 0.0) & (rr < MOE_CAP), e * MOE_CAP + rr, -1.0)

        tgt = jnp.concatenate([target(e0, r0), target(e1, r1)], axis=0)
        sel = jnp.where(lane == tgt, 1.0, 0.0).astype(BF16)
        picked = jnp.dot(sel, yp, preferred_element_type=F32)
        return w0 * picked[:ROW_TILE] + w1 * picked[ROW_TILE:]

    acc_ref[...] = gathered(0)

    def extra_round(rnd, carry):
        stage(tile, slot, rnd, True)
        stage(tile, slot, rnd, False)
        acc_ref[...] += gathered(rnd.astype(F32))
        return carry

    lax.fori_loop(1, nround_ref[tile], extra_round, 0)
    gate2 = mod_ref[0][:, 5 * D_MODEL:6 * D_MODEL]
    o_ref[0] = x_ref[0] + gate2 * acc_ref[...]


def _combine(tables, info_cols, xs, mod, ys):
    B, N, _ = xs.shape
    nt = N // ROW_TILE
    grid_spec = pltpu.PrefetchScalarGridSpec(
        num_scalar_prefetch=3,
        grid=(B, nt),
        in_specs=[pl.BlockSpec((1, ROW_TILE, 8), lambda b, i, *_: (b * nt + i, 0, 0)),
                  pl.BlockSpec((1, ROW_TILE, D_MODEL), lambda b, i, *_: (b, i, 0)),
                  pl.BlockSpec((1, 1, 6 * D_MODEL), lambda b, i, *_: (jnp.where(i == 0, B, b), 0, 0)),
                  pl.BlockSpec(memory_space=pl.ANY)],
        out_specs=pl.BlockSpec((1, ROW_TILE, D_MODEL), lambda b, i, *_: (b, i, 0)),
        scratch_shapes=[pltpu.VMEM((2, N_EXPERTS * MOE_CAP, D_MODEL // 2), jnp.uint32),
                        pltpu.VMEM((ROW_TILE, D_MODEL), F32),
                        pltpu.SemaphoreType.DMA((2,))],
    )
    return pl.pallas_call(
        _combine_kernel,
        out_shape=jax.ShapeDtypeStruct((B, N, D_MODEL), F32),
        grid_spec=grid_spec,
        compiler_params=_cparams("arbitrary", "arbitrary"),
        name="moe_combine",
    )(tables["dst"], tables["nch"], tables["nround"], info_cols, xs, mod, ys)


def _moe_static_rows(n_tokens):
    n_tiles = n_tokens // ROW_TILE
    worst = n_tokens * TOP_K + n_tiles * N_EXPERTS * (MOE_ALIGN - 1) + N_EXPERTS * (2 * MOE_BLOCK - 1)
    return -(-worst // MOE_BLOCK) * MOE_BLOCK


def _moe_tables(counts, n_rows_static):
    n_tiles = counts.shape[0]
    cnt = counts.astype(jnp.int32)
    cnt_al = -(-cnt // MOE_ALIGN) * MOE_ALIGN
    seg_off = jnp.cumsum(cnt_al, axis=0) - cnt_al
    tot = jnp.sum(cnt_al, axis=0)
    nblk = -(-tot // MOE_BLOCK)
    esize = (nblk + 1) * MOE_BLOCK
    eend = jnp.cumsum(esize)
    estart = eend - esize
    loc = jnp.cumsum(cnt_al, axis=1) - cnt_al
    nch = -(-cnt_al // MOE_CAP)
    blk0 = jnp.arange(n_rows_static // MOE_BLOCK, dtype=jnp.int32) * MOE_BLOCK
    bexp = jnp.minimum(jnp.sum(blk0[:, None] >= eend[None, :], axis=1), N_EXPERTS - 1).astype(jnp.int32)
    in_expert = blk0 - jnp.sum(jnp.where(blk0[:, None] >= eend[None, :], esize[None, :], 0), axis=1)
    n_used = jnp.sum(jnp.where(jnp.arange(N_EXPERTS)[None, :] == bexp[:, None], nblk[None, :], 0), axis=1)
    used = (in_expert < n_used * MOE_BLOCK) & (blk0 < eend[-1])
    blk_id = jnp.arange(blk0.shape[0], dtype=jnp.int32)
    xblk = jnp.where(used, blk_id, jnp.maximum(blk_id - (in_expert // MOE_BLOCK - n_used) - 1, 0))
    xblk = jnp.where(blk0 < eend[-1], xblk, 0)
    i32 = lambda a: a.astype(jnp.int32)
    return {
        "dst": i32((estart[None, :] + seg_off).reshape(-1)),
        "loc": i32(loc.reshape(-1)),
        "nch": i32(nch.reshape(-1)),
        "nround": i32(jnp.maximum(jnp.max(nch, axis=1), 1)),
        "zfill": i32(~used | (in_expert == (n_used - 1) * MOE_BLOCK)),
        "first": i32(used & (in_expert == 0)),
        "bexp": bexp, "used": i32(used), "xblk": i32(xblk),
    }


def _rope_tables(n_ctx, n_lat):
    row = jnp.repeat(jnp.arange(n_lat // GRID_W, dtype=F32), GRID_W)
    col = jnp.tile(jnp.arange(GRID_W, dtype=F32), n_lat // GRID_W)
    inv = ROPE_BASE ** (-jnp.arange(0, ROPE_AXIS_DIM, 2, dtype=F32) / ROPE_AXIS_DIM)
    ang = jnp.concatenate([row[:, None] * inv, col[:, None] * inv], axis=-1)
    cos = jnp.concatenate([jnp.ones((n_ctx, ROPE_AXIS_DIM), F32), jnp.cos(ang)], axis=0)
    sin = jnp.concatenate([jnp.zeros((n_ctx, ROPE_AXIS_DIM), F32), jnp.sin(ang)], axis=0)
    return jnp.tile(cos, (1, 4)), jnp.tile(jnp.concatenate([-sin, sin], axis=1), (1, 2))


def _block_diag(w):
    g, n, _ = w.shape
    out = jnp.zeros((g * n, g * n), w.dtype)
    for i in range(g):
        out = out.at[i * n:(i + 1) * n, i * n:(i + 1) * n].set(w[i])
    return out


def kernel(x, c, ctx, c_ctx, norm1_g, norm2_g, ada_w, ada_b, w_in, w_out, pool_w, pool_scale,
           q_norm_g, k_norm_g, attn_sink, hgrn_lb, hgrn_norm_g, router_group_w, router_group_b,
           router_expert_w, router_expert_b, expert_w_gate, expert_w_up, expert_w_down):
    B, L, _ = x.shape
    C = ctx.shape[1]
    N = C + L
    depth = ada_w.shape[0]
    assert C == ROW_TILE and L % ROW_TILE == 0 and L % GRID_W == 0

    cos2, sin2 = _rope_tables(C, L)
    sm = jax.nn.softmax(hgrn_lb.astype(F32), axis=0)
    lb_all = jnp.cumsum(sm, axis=0) - sm[0]

    craw = jnp.concatenate([c, c_ctx[None], jnp.zeros((8 - B - 1, D_MODEL), F32)], axis=0)
    mod_all = _ada(craw, ada_w, ada_b)[:, :B + 1].reshape(depth, B + 1, 1, 6 * D_MODEL)

    xs = jnp.concatenate([ctx, x], axis=1)
    for l in range(depth):
        mod = mod_all[l]
        u, q, k, v, hq, hi, hzf, hzb, hgate = _in_proj(xs, norm1_g[l], mod, w_in[l].astype(BF16))
        a = _pool(u, _block_diag(pool_w[l]).astype(BF16), pool_scale[l], C)
        qh, kh, vh = _qk_prep(q, k, v, cos2, sin2,
                              jnp.tile(q_norm_g[l].reshape(1, HEAD_DIM), (1, 2)),
                              jnp.tile(k_norm_g[l].reshape(1, HEAD_DIM), (1, 2)))
        b_mix = _attention(qh, kh, vh, attn_sink[l], C)
        o_f = _hgrn_scan(hq, hi, hzf, lb_all[l, 0], reverse=False)
        c_mix = _hgrn_scan(hq, hi, hzb, lb_all[l, 1], reverse=True, o_fwd=o_f, hgate=hgate,
                           norm_g=hgrn_norm_g[l])
        pad_cols = lambda n: jnp.zeros((D_MODEL, n), F32)
        r_w = jnp.concatenate([router_group_w[l], pad_cols(ROUTER_EXPERT_ROW0 - N_GROUPS),
                               router_expert_w[l],
                               pad_cols(ROUTER_ROWS - ROUTER_EXPERT_ROW0 - N_EXPERTS)], axis=1)
        r_b = jnp.concatenate([router_group_b[l], jnp.zeros((ROUTER_EXPERT_ROW0 - N_GROUPS,), F32),
                               router_expert_b[l],
                               jnp.zeros((ROUTER_ROWS - ROUTER_EXPERT_ROW0 - N_EXPERTS,), F32)])
        r_hi = r_w.astype(BF16)
        r_lo = (r_w - r_hi.astype(F32)).astype(BF16)
        xs, h2, logits_t = _out_proj(a, b_mix, c_mix, xs, w_out[l].astype(BF16), mod, norm2_g[l],
                                     r_hi, r_lo, r_b.reshape(1, ROUTER_ROWS))
        info, counts = _route(logits_t)
        n_rows_static = _moe_static_rows(B * N)
        tables = _moe_tables(counts[:, :, 0], n_rows_static)
        xs_sorted = _dispatch(tables, info, h2.reshape(B * N, D_MODEL), n_rows_static)
        ys = _expert_ffn(tables, xs_sorted, expert_w_gate, expert_w_up, expert_w_down, l)
        xs = _combine(tables, jnp.swapaxes(info, 1, 2), xs, mod, ys)
    return xs[:, C:]
```

```python
import functools

import numpy as np
import jax
import jax.numpy as jnp
from jax import lax
from jax.experimental import pallas as pl
from jax.experimental.pallas import tpu as pltpu

D_MODEL = 1024
DEPTH = 4
GRID_W = 64
NORM_EPS = 1e-6
NEG_INF = -1e30

POOL_WINDOWS = (2, 4, 8, 16)
POOL_GROUP_DIM = 64
POOL_DIM = 256

HEAD_DIM = 64
ATTN_HEADS = 8
ATTN_KV_HEADS = 2
Q_PER_KV = 4
ATTN_DIM = 512
KV_DIM = 128
ATTN_WINDOW = 128
ATTN_BLOCK = 128
ROPE_BASE = 10000.0
ROPE_AXIS_DIM = 32

HG_HEADS = 4
HG_DIM = 64
HG_WIDTH = 256
HG_CHUNK = 64
HG_LEVELS = 7
HG_BATCH_STEP = 2

MIX_WIDTH = 1024
IN_SIZES = (POOL_DIM, ATTN_DIM, KV_DIM, KV_DIM, HG_WIDTH, HG_WIDTH, HG_WIDTH, HG_WIDTH, HG_WIDTH)
IN_OFFS = tuple(int(sum(IN_SIZES[:i])) for i in range(len(IN_SIZES) + 1))
IN_COLS = IN_OFFS[-1]

N_GROUPS = 4
EXPERTS_PER_GROUP = 8
N_EXPERTS = 32
TOP_K = 2
EXPERT_HIDDEN = 512
MOE_BLOCK = 256
ROUTER_ROWS = 128
ROUTER_EXPERT_ROW0 = 8
MOE_ALIGN = 8
MOE_CAP = 32
MOE_STAGE_SLOTS = 3

ROW_TILE = 256
MOE_TILE_SLOTS = TOP_K * ROW_TILE + N_EXPERTS * MOE_ALIGN
ADA_COL_TILE = 1536
VMEM_LIMIT = 56 * 1024 * 1024

F32 = jnp.float32
BF16 = jnp.bfloat16


def _cparams(*sem):
    return pltpu.CompilerParams(dimension_semantics=sem, vmem_limit_bytes=VMEM_LIMIT)


def _rms_scale(x):
    return lax.rsqrt(jnp.mean(x * x, axis=-1, keepdims=True) + NORM_EPS)


def _pack_pairs(x):
    n = x.shape[1] // 2
    lo = lax.bitcast_convert_type(x[:, :n], jnp.uint32) >> 16
    hi = lax.bitcast_convert_type(x[:, n:], jnp.uint32) & jnp.uint32(0xFFFF0000)
    return lo | hi


def _unpack_pairs(u):
    lo = lax.bitcast_convert_type(u << 16, F32)
    hi = lax.bitcast_convert_type(u & jnp.uint32(0xFFFF0000), F32)
    return jnp.concatenate([lo, hi], axis=1)


def _ada_kernel(c_ref, w_ref, b_ref, o_ref):
    s = c_ref[...]
    s = (s * jax.nn.sigmoid(s)).astype(BF16)
    o_ref[0] = jnp.dot(s, w_ref[0].astype(BF16), preferred_element_type=F32) + b_ref[0]


def _ada(craw, ada_w, ada_b):
    depth = ada_w.shape[0]
    ncol = ada_w.shape[2]
    return pl.pallas_call(
        _ada_kernel,
        out_shape=jax.ShapeDtypeStruct((depth, 8, ncol), F32),
        grid=(depth, ncol // ADA_COL_TILE),
        in_specs=[
            pl.BlockSpec((8, D_MODEL), lambda l, j: (0, 0)),
            pl.BlockSpec((1, D_MODEL, ADA_COL_TILE), lambda l, j: (l, 0, j)),
            pl.BlockSpec((1, 1, ADA_COL_TILE), lambda l, j: (l, 0, j)),
        ],
        out_specs=pl.BlockSpec((1, 8, ADA_COL_TILE), lambda l, j: (l, 0, j)),
        compiler_params=_cparams("arbitrary", "arbitrary"),
        name="ada_mod",
    )(craw, ada_w, ada_b.reshape(depth, 1, ncol))


def _in_proj_kernel(x_ref, g_ref, mod_ref, w_ref, *out_refs):
    x = x_ref[0]
    mod = mod_ref[0]
    sh = mod[:, 0:D_MODEL]
    sc = mod[:, D_MODEL:2 * D_MODEL]
    h = (x * _rms_scale(x) * g_ref[...] * (1.0 + sc) + sh).astype(BF16)
    for o_ref, lo, hi in zip(out_refs, IN_OFFS[:-1], IN_OFFS[1:]):
        o_ref[0] = jnp.dot(h, w_ref[:, lo:hi], preferred_element_type=F32)


def _mod_spec(n_batch):
    return pl.BlockSpec((1, 1, 6 * D_MODEL), lambda b, i: (jnp.where(i == 0, n_batch, b), 0, 0))


def _in_proj(xs, g1, mod, w_in_bf16):
    B, N, _ = xs.shape
    nt = N // ROW_TILE
    return pl.pallas_call(
        _in_proj_kernel,
        out_shape=[jax.ShapeDtypeStruct((B, N, w), F32) for w in IN_SIZES],
        grid=(B, nt),
        in_specs=[
            pl.BlockSpec((1, ROW_TILE, D_MODEL), lambda b, i: (b, i, 0)),
            pl.BlockSpec((1, D_MODEL), lambda b, i: (0, 0)),
            _mod_spec(B),
            pl.BlockSpec((D_MODEL, IN_COLS), lambda b, i: (0, 0)),
        ],
        out_specs=[pl.BlockSpec((1, ROW_TILE, w), lambda b, i: (b, i, 0)) for w in IN_SIZES],
        compiler_params=_cparams("arbitrary", "arbitrary"),
        name="in_proj",
    )(xs, g1.reshape(1, D_MODEL), mod, w_in_bf16)


def _qk_prep_kernel(q_ref, k_ref, v_ref, cos_ref, sin_ref, gq_ref, gk_ref, qh_ref, kh_ref, vh_ref):
    cos = cos_ref[...]
    sin = sin_ref[...]
    lane = lax.broadcasted_iota(jnp.int32, (1, 2 * HEAD_DIM), 1)
    lo_head = lane < HEAD_DIM
    first_half = (lane % HEAD_DIM) < (HEAD_DIM // 2)

    def norm_rope(x2, g, scale):
        sq = x2 * x2
        s0 = jnp.sum(jnp.where(lo_head, sq, 0.0), axis=-1, keepdims=True)
        s1 = jnp.sum(jnp.where(lo_head, 0.0, sq), axis=-1, keepdims=True)
        ms = jnp.where(lo_head, s0, s1) * (1.0 / HEAD_DIM)
        xn = x2 * lax.rsqrt(ms + NORM_EPS) * g
        swapped = jnp.where(first_half, pltpu.roll(xn, 2 * HEAD_DIM - HEAD_DIM // 2, 1),
                            pltpu.roll(xn, HEAD_DIM // 2, 1))
        return (xn * cos + swapped * sin) * scale

    for p in range(ATTN_HEADS // 2):
        y = norm_rope(q_ref[0, :, 2 * HEAD_DIM * p:2 * HEAD_DIM * (p + 1)], gq_ref[...], HEAD_DIM ** -0.5)
        qh_ref[0, 2 * p] = y[:, :HEAD_DIM].astype(BF16)
        qh_ref[0, 2 * p + 1] = y[:, HEAD_DIM:].astype(BF16)
    y = norm_rope(k_ref[0], gk_ref[...], 1.0)
    kh_ref[0, 0] = y[:, :HEAD_DIM].astype(BF16)
    kh_ref[0, 1] = y[:, HEAD_DIM:].astype(BF16)
    v = v_ref[0]
    vh_ref[0, 0] = v[:, :HEAD_DIM].astype(BF16)
    vh_ref[0, 1] = v[:, HEAD_DIM:].astype(BF16)


def _qk_prep(q, k, v, cos2, sin2, gq2, gk2):
    B, N, _ = q.shape
    nt = N // ROW_TILE
    row = lambda w: pl.BlockSpec((1, ROW_TILE, w), lambda b, i: (b, i, 0))
    tab = pl.BlockSpec((ROW_TILE, 2 * HEAD_DIM), lambda b, i: (i, 0))
    vec = pl.BlockSpec((1, 2 * HEAD_DIM), lambda b, i: (0, 0))
    head = lambda n: pl.BlockSpec((1, n, ROW_TILE, HEAD_DIM), lambda b, i: (b, 0, i, 0))
    return pl.pallas_call(
        _qk_prep_kernel,
        out_shape=[jax.ShapeDtypeStruct((B, ATTN_HEADS, N, HEAD_DIM), BF16),
                   jax.ShapeDtypeStruct((B, ATTN_KV_HEADS, N, HEAD_DIM), BF16),
                   jax.ShapeDtypeStruct((B, ATTN_KV_HEADS, N, HEAD_DIM), BF16)],
        grid=(B, nt),
        in_specs=[row(ATTN_DIM), row(KV_DIM), row(KV_DIM), tab, tab, vec, vec],
        out_specs=[head(ATTN_HEADS), head(ATTN_KV_HEADS), head(ATTN_KV_HEADS)],
        compiler_params=_cparams("arbitrary", "arbitrary"),
        name="qk_prep",
    )(q, k, v, cos2, sin2, gq2, gk2)


def _attn_kernel(sink_ref, q_ref, kp_ref, kc_ref, kn_ref, kx_ref, vp_ref, vc_ref, vn_ref, vx_ref,
                 o_ref, *, n_ctx, n_rows):
    i = pl.program_id(1)
    blk = ATTN_BLOCK
    t = i * blk + lax.broadcasted_iota(jnp.int32, (blk, 1), 0)
    kr = (i - 1) * blk + lax.broadcasted_iota(jnp.int32, (1, 3 * blk), 1)
    ok_loc = (t >= n_ctx) & (kr >= n_ctx) & (kr < n_rows) & (jnp.abs(t - kr) <= ATTN_WINDOW)
    bias = jnp.concatenate([jnp.where(ok_loc, 0.0, NEG_INF), jnp.zeros((blk, n_ctx), F32)],
                           axis=1)[None]
    heads = []
    for g in range(ATTN_KV_HEADS):
        q = q_ref[0, Q_PER_KV * g:Q_PER_KV * (g + 1)].reshape(Q_PER_KV * blk, HEAD_DIM)
        k = jnp.concatenate([kp_ref[0, g], kc_ref[0, g], kn_ref[0, g], kx_ref[0, g]], axis=0)
        v = jnp.concatenate([vp_ref[0, g], vc_ref[0, g], vn_ref[0, g], vx_ref[0, g]], axis=0)
        s = lax.dot_general(q, k, (((1,), (1,)), ((), ())), preferred_element_type=F32)
        nk = s.shape[-1]
        s = (s.reshape(Q_PER_KV, blk, nk) + bias).reshape(Q_PER_KV * blk, nk)
        sk = jnp.concatenate(
            [jnp.full((blk, 1), sink_ref[Q_PER_KV * g + hh], F32) for hh in range(Q_PER_KV)], axis=0)
        m = jnp.maximum(jnp.max(s, axis=-1, keepdims=True), sk)
        p = jnp.exp(s - m)
        denom = jnp.sum(p, axis=-1, keepdims=True) + jnp.exp(sk - m)
        o = jnp.dot(p.astype(BF16), v, preferred_element_type=F32) / denom
        heads += [o[hh * blk:(hh + 1) * blk] for hh in range(Q_PER_KV)]
    o_ref[0] = jnp.concatenate(heads, axis=1).astype(BF16)


def _attention(qh, kh, vh, sink, n_ctx):
    B, _, N, _ = qh.shape
    nq = N // ATTN_BLOCK
    kv = lambda f: pl.BlockSpec((1, ATTN_KV_HEADS, ATTN_BLOCK, HEAD_DIM), f)
    prev = lambda b, i: (b, 0, jnp.maximum(i - 1, 0), 0)
    cur = lambda b, i: (b, 0, i, 0)
    nxt = lambda b, i: (b, 0, jnp.minimum(i + 1, nq - 1), 0)
    ctx = pl.BlockSpec((1, ATTN_KV_HEADS, n_ctx, HEAD_DIM), lambda b, i: (b, 0, 0, 0))
    return pl.pallas_call(
        functools.partial(_attn_kernel, n_ctx=n_ctx, n_rows=N),
        out_shape=jax.ShapeDtypeStruct((B, N, ATTN_DIM), BF16),
        grid=(B, nq),
        in_specs=[pl.BlockSpec(memory_space=pltpu.SMEM),
                  pl.BlockSpec((1, ATTN_HEADS, ATTN_BLOCK, HEAD_DIM), cur),
                  kv(prev), kv(cur), kv(nxt), ctx, kv(prev), kv(cur), kv(nxt), ctx],
        out_specs=pl.BlockSpec((1, ATTN_BLOCK, ATTN_DIM), lambda b, i: (b, i, 0)),
        compiler_params=_cparams("arbitrary", "arbitrary"),
        name="band_attention",
    )(sink, qh, kh, kh, kh, kh, vh, vh, vh, vh)


def _hgrn_tables(reverse):
    c = HG_CHUNK
    t = np.arange(c)
    msum = np.zeros((HG_LEVELS + 1, c, c), np.float32)
    qm = np.zeros((HG_LEVELS, c, 1), np.float32)
    km = np.zeros((HG_LEVELS, c, 1), np.float32)
    sm = np.zeros((HG_LEVELS, c, c), np.float32)
    qm[0] = 1.0
    km[0] = 1.0
    sm[0] = np.eye(c)
    for j in range(1, HG_LEVELS):
        m = c >> (j - 1)
        half = m // 2
        later = (t % m) >= half
        mid = (t // m) * m + half
        r = t[None, :]
        rows_later = later[:, None] & (r > mid[:, None]) & (r <= t[:, None])
        rows_early = (~later)[:, None] & (r > t[:, None]) & (r <= mid[:, None])
        msum[j - 1] = (rows_later | rows_early).astype(np.float32)
        qm[j, :, 0] = later
        km[j, :, 0] = ~later
        sm[j] = ((t[:, None] // m) == (t[None, :] // m)).astype(np.float32)
    msum[HG_LEVELS - 1] = (t[None, :] <= t[:, None]).astype(np.float32)
    msum[HG_LEVELS] = (t[None, :] > t[:, None]).astype(np.float32)
    if reverse:
        msum = msum[:, ::-1, ::-1]
        qm = qm[:, ::-1]
        km = km[:, ::-1]
        sm = sm[:, ::-1, ::-1]
    ones = np.ones((1, 1, HG_WIDTH), np.float32)
    return (jnp.asarray(msum.reshape(-1, c), BF16),
            jnp.asarray(qm * ones), jnp.asarray(km * ones),
            jnp.asarray(np.tile(sm, (1, 1, HG_HEADS))))


def _head_mask():
    h = np.arange(HG_WIDTH) // HG_DIM
    return (h[:, None] == h[None, :]).astype(np.float32)


def _hgrn_kernel(*refs, reverse):
    if reverse:
        (hq_ref, hi_ref, hz_ref, lb_ref, ms_ref, qm_ref, km_ref, sm_ref, hm_ref,
         of_ref, hg_ref, hmean_ref, ng_ref, o_ref, st_ref) = refs
    else:
        (hq_ref, hi_ref, hz_ref, lb_ref, ms_ref, qm_ref, km_ref, sm_ref, hm_ref,
         o_ref, st_ref) = refs
    c = HG_CHUNK
    n_chunks = ROW_TILE // c

    @pl.when(pl.program_id(1) == 0)
    def _():
        st_ref[...] = jnp.zeros_like(st_ref)

    lb = lb_ref[...]
    hm_bf = hm_ref[...]
    hm = hm_bf.astype(F32)
    outs = [[None] * n_chunks for _ in range(HG_BATCH_STEP)]
    order = range(n_chunks - 1, -1, -1) if reverse else range(n_chunks)
    for ci, bb in [(ci, bb) for ci in order for bb in range(HG_BATCH_STEP)]:
        rows = slice(ci * c, (ci + 1) * c)
        hq = hq_ref[bb, rows, :]
        q = hq * jax.nn.sigmoid(hq)
        v = hi_ref[bb, rows, :]
        z = hz_ref[bb, rows, :]
        g = jnp.log(lb + (1.0 - lb) * jax.nn.sigmoid(z))
        kk = (1.0 - lb) * jax.nn.sigmoid(-z)
        g_hi = g.astype(BF16)
        g_lo = (g - g_hi.astype(F32)).astype(BF16)
        dsum = jnp.dot(ms_ref[...], jnp.concatenate([g_hi, g_lo], axis=1), preferred_element_type=F32)
        e = jnp.exp(dsum[:, :HG_WIDTH] + dsum[:, HG_WIDTH:])
        scores = jnp.zeros((c, HG_HEADS * c), F32)
        for lv in range(HG_LEVELS):
            if lv == 0:
                ql, kl = q, kk
            else:
                el = e[(lv - 1) * c:lv * c]
                ql = q * el * qm_ref[lv]
                kl = kk * el * km_ref[lv]
            kbd = jnp.concatenate([kl.astype(BF16)] * HG_HEADS, axis=0) * hm_bf
            s_l = lax.dot_general(ql.astype(BF16), kbd, (((1,), (1,)), ((), ())),
                                  preferred_element_type=F32)
            scores = scores + s_l * sm_ref[lv]
        vbd = jnp.concatenate([v.astype(BF16)] * HG_HEADS, axis=0) * hm_bf
        o = jnp.dot(scores.astype(BF16), vbd, preferred_element_type=F32)
        e_cum = e[(HG_LEVELS - 1) * c:HG_LEVELS * c]
        st = st_ref[bb]
        o = o + lax.dot_general((q * e_cum).astype(BF16), st.astype(BF16), (((1,), (1,)), ((), ())),
                                preferred_element_type=F32)
        k_dec = kk * e[HG_LEVELS * c:(HG_LEVELS + 1) * c]
        e_end = e_cum[0:1] if reverse else e_cum[c - 1:c]
        upd = jnp.dot(v.T.astype(BF16), k_dec.astype(BF16), preferred_element_type=F32)
        st_ref[bb] = st * e_end + upd * hm
        outs[bb][ci] = o
    for bb in range(HG_BATCH_STEP):
        o_all = jnp.concatenate(outs[bb], axis=0)
        if reverse:
            o_sum = of_ref[bb] + o_all
            ms = jnp.dot((o_sum * o_sum).astype(BF16), hmean_ref[...], preferred_element_type=F32)
            gate = hg_ref[bb]
            o_ref[bb] = (o_sum * lax.rsqrt(ms + NORM_EPS) * ng_ref[...]
                         * (gate * jax.nn.sigmoid(gate))).astype(BF16)
        else:
            o_ref[bb] = o_all


def _hgrn_scan(hq, hi, hz, lb, reverse, o_fwd=None, hgate=None, norm_g=None):
    B, N, _ = hq.shape
    nt = N // ROW_TILE
    if reverse:
        order = lambda b, i: (b, jnp.where(i == 0, 0, nt - i), 0)
    else:
        order = lambda b, i: (b, i, 0)
    assert B % HG_BATCH_STEP == 0
    row = pl.BlockSpec((HG_BATCH_STEP, ROW_TILE, HG_WIDTH), order)
    const = lambda a: pl.BlockSpec(a.shape, lambda b, i: (0,) * a.ndim)
    msum, qm, km, sm = _hgrn_tables(reverse)
    hm = jnp.asarray(_head_mask(), BF16)
    args = [hq, hi, hz, lb.reshape(1, HG_WIDTH), msum, qm, km, sm, hm]
    specs = [row, row, row] + [const(a) for a in args[3:]]
    if reverse:
        hmean = jnp.asarray(_head_mask() / HG_DIM, BF16)
        ng = jnp.tile(norm_g.reshape(1, HG_DIM), (1, HG_HEADS))
        args += [o_fwd, hgate, hmean, ng]
        specs += [row, row, const(hmean), const(ng)]
    return pl.pallas_call(
        functools.partial(_hgrn_kernel, reverse=reverse),
        out_shape=jax.ShapeDtypeStruct((B, N, HG_WIDTH), BF16 if reverse else F32),
        grid=(B // HG_BATCH_STEP, nt),
        in_specs=specs,
        out_specs=row,
        scratch_shapes=[pltpu.VMEM((HG_BATCH_STEP, HG_WIDTH, HG_WIDTH), F32)],
        compiler_params=_cparams("arbitrary", "arbitrary"),
        name="hgrn_bwd" if reverse else "hgrn_fwd",
    )(*args)


POOL_HALO = 8


def _pool_kernel(up_ref, u_ref, un_ref, w_ref, sc_ref, o_ref, e_ref, *, n_ctx, n_lat):
    i = pl.program_id(1)
    nt = pl.num_programs(1)
    u = u_ref[0]
    has_prev = i >= 2
    has_next = (i >= 1) & (i <= nt - 2)
    e_ref[0:POOL_HALO, :] = jnp.where(has_prev, up_ref[0], 0.0)
    e_ref[POOL_HALO:POOL_HALO + ROW_TILE, :] = u
    e_ref[POOL_HALO + ROW_TILE:, :] = jnp.where(has_next, un_ref[0], 0.0)
    seg_start = jnp.where(i == 0, 0, n_ctx)
    seg_len = jnp.where(i == 0, n_ctx, n_lat)
    tau = i * ROW_TILE - seg_start + lax.broadcasted_iota(jnp.int32, (ROW_TILE, 1), 0)
    lane_grp = lax.broadcasted_iota(jnp.int32, (1, 2 * POOL_GROUP_DIM), 1) // POOL_GROUP_DIM
    halves = []
    for hb in range(2):
        w_a, w_b = POOL_WINDOWS[2 * hb], POOL_WINDOWS[2 * hb + 1]
        lanes = slice(2 * POOL_GROUP_DIM * hb, 2 * POOL_GROUP_DIM * (hb + 1))
        acc = jnp.zeros((ROW_TILE, 2 * POOL_GROUP_DIM), F32)
        for off in range(-((w_b - 1) // 2), w_b // 2 + 1):
            x = e_ref[POOL_HALO + off:POOL_HALO + off + ROW_TILE, lanes]
            if -((w_a - 1) // 2) <= off <= w_a // 2:
                acc = acc + x
            else:
                acc = acc + jnp.where(lane_grp == 1, x, 0.0)

        def count(w):
            lo = jnp.maximum(tau - (w - 1) // 2, 0)
            hi = jnp.minimum(tau + w // 2 + 1, seg_len)
            return (hi - lo).astype(F32)

        cnt = jnp.where(lane_grp == 0, count(w_a), count(w_b))
        halves.append(acc / cnt)
    pooled = jnp.concatenate(halves, axis=1) - u
    y = jnp.dot(pooled.astype(BF16), w_ref[...], preferred_element_type=F32) * sc_ref[...]
    o_ref[0] = y.astype(BF16)


def _pool(u, w_bd, scale, n_ctx):
    B, N, _ = u.shape
    nt = N // ROW_TILE
    per = ROW_TILE // POOL_HALO
    nh = N // POOL_HALO
    return pl.pallas_call(
        functools.partial(_pool_kernel, n_ctx=n_ctx, n_lat=N - n_ctx),
        out_shape=jax.ShapeDtypeStruct((B, N, POOL_DIM), BF16),
        grid=(B, nt),
        in_specs=[
            pl.BlockSpec((1, POOL_HALO, POOL_DIM), lambda b, i: (b, jnp.maximum(i * per - 1, 0), 0)),
            pl.BlockSpec((1, ROW_TILE, POOL_DIM), lambda b, i: (b, i, 0)),
            pl.BlockSpec((1, POOL_HALO, POOL_DIM), lambda b, i: (b, jnp.minimum((i + 1) * per, nh - 1), 0)),
            pl.BlockSpec((POOL_DIM, POOL_DIM), lambda b, i: (0, 0)),
            pl.BlockSpec((1, POOL_DIM), lambda b, i: (0, 0)),
        ],
        out_specs=pl.BlockSpec((1, ROW_TILE, POOL_DIM), lambda b, i: (b, i, 0)),
        scratch_shapes=[pltpu.VMEM((ROW_TILE + 2 * POOL_HALO, POOL_DIM), F32)],
        compiler_params=_cparams("arbitrary", "arbitrary"),
        name="pool_mixer",
    )(u, u, u, w_bd, scale.reshape(1, POOL_DIM))


def _out_proj_kernel(a_ref, b_ref, c_ref, x_ref, w_ref, mod_ref, g2_ref, rh_ref, rl_ref, rb_ref,
                     xo_ref, h2_ref, lg_ref):
    mix = (jnp.dot(a_ref[0], w_ref[0:POOL_DIM, :], preferred_element_type=F32)
           + jnp.dot(b_ref[0], w_ref[POOL_DIM:POOL_DIM + ATTN_DIM, :], preferred_element_type=F32)
           + jnp.dot(c_ref[0], w_ref[POOL_DIM + ATTN_DIM:, :], preferred_element_type=F32))
    mod = mod_ref[0]
    gate1 = mod[:, 2 * D_MODEL:3 * D_MODEL]
    sh2 = mod[:, 3 * D_MODEL:4 * D_MODEL]
    sc2 = mod[:, 4 * D_MODEL:5 * D_MODEL]
    x = x_ref[0] + gate1 * mix
    xo_ref[0] = x
    h2 = x * _rms_scale(x) * g2_ref[...] * (1.0 + sc2) + sh2
    h2_ref[0] = h2
    h_hi = h2.astype(BF16)
    h_lo = (h2 - h_hi.astype(F32)).astype(BF16)
    lg = (jnp.dot(h_hi, rh_ref[...], preferred_element_type=F32)
          + jnp.dot(h_lo, rh_ref[...], preferred_element_type=F32)
          + jnp.dot(h_hi, rl_ref[...], preferred_element_type=F32) + rb_ref[...])
    lg_ref[0] = lg.T


def _out_proj(a, b, c, xs, w_out_bf16, mod, g2, r_hi, r_lo, r_b):
    B, N, _ = xs.shape
    nt = N // ROW_TILE
    row = lambda w: pl.BlockSpec((1, ROW_TILE, w), lambda b_, i: (b_, i, 0))
    full = lambda s: pl.BlockSpec(s, lambda b_, i: (0,) * len(s))
    return pl.pallas_call(
        _out_proj_kernel,
        out_shape=[jax.ShapeDtypeStruct((B, N, D_MODEL), F32),
                   jax.ShapeDtypeStruct((B, N, D_MODEL), F32),
                   jax.ShapeDtypeStruct((B * nt, ROUTER_ROWS, ROW_TILE), F32)],
        grid=(B, nt),
        in_specs=[row(POOL_DIM), row(ATTN_DIM), row(HG_WIDTH), row(D_MODEL),
                  full((MIX_WIDTH, D_MODEL)), _mod_spec(B), full((1, D_MODEL)),
                  full((D_MODEL, ROUTER_ROWS)), full((D_MODEL, ROUTER_ROWS)), full((1, ROUTER_ROWS))],
        out_specs=[row(D_MODEL), row(D_MODEL),
                   pl.BlockSpec((1, ROUTER_ROWS, ROW_TILE), lambda b_, i: (b_ * nt + i, 0, 0))],
        compiler_params=_cparams("arbitrary", "arbitrary"),
        name="out_proj_router",
    )(a, b, c, xs, w_out_bf16, mod, g2.reshape(1, D_MODEL), r_hi, r_lo, r_b)


def _route_kernel(lg_ref, info_ref, cnt_ref):
    lgt = lg_ref[0]
    tile = lgt.shape[1]
    lg = lgt[0:N_GROUPS]
    gmax = jnp.max(lg, axis=0, keepdims=True)
    ridx = lax.broadcasted_iota(jnp.int32, lg.shape, 0)
    g_idx = jnp.min(jnp.where(lg == gmax, ridx, N_GROUPS), axis=0, keepdims=True)
    p_grp = 1.0 / jnp.sum(jnp.exp(lg - gmax), axis=0, keepdims=True)
    le = jnp.zeros((EXPERTS_PER_GROUP, tile), F32)
    for gi in range(N_GROUPS):
        lo = ROUTER_EXPERT_ROW0 + gi * EXPERTS_PER_GROUP
        le = jnp.where(g_idx == gi, lgt[lo:lo + EXPERTS_PER_GROUP], le)
    ex = jnp.exp(le - jnp.max(le, axis=0, keepdims=True))
    pe = ex / jnp.sum(ex, axis=0, keepdims=True)
    ridx = lax.broadcasted_iota(jnp.int32, pe.shape, 0)
    m1 = jnp.max(pe, axis=0, keepdims=True)
    i1 = jnp.min(jnp.where(pe == m1, ridx, EXPERTS_PER_GROUP), axis=0, keepdims=True)
    pe2 = jnp.where(ridx == i1, -1.0, pe)
    m2 = jnp.max(pe2, axis=0, keepdims=True)
    i2 = jnp.min(jnp.where(pe2 == m2, ridx, EXPERTS_PER_GROUP), axis=0, keepdims=True)
    tot = m1 + m2
    w0 = p_grp * (m1 / tot)
    w1 = p_grp * (m2 / tot)
    e0 = g_idx * EXPERTS_PER_GROUP + i1
    e1 = g_idx * EXPERTS_PER_GROUP + i2
    eidx = lax.broadcasted_iota(jnp.int32, (N_EXPERTS, tile), 0)
    oh0 = jnp.where(eidx == e0, 1.0, 0.0)
    oh1 = jnp.where(eidx == e1, 1.0, 0.0)
    before = jnp.where(lax.broadcasted_iota(jnp.int32, (tile, tile), 0)
                       < lax.broadcasted_iota(jnp.int32, (tile, tile), 1), 1.0, 0.0).astype(BF16)
    c0 = jnp.dot(oh0.astype(BF16), before, preferred_element_type=F32)
    c1 = jnp.dot(oh1.astype(BF16), before, preferred_element_type=F32)
    cnt0 = jnp.sum(oh0, axis=1, keepdims=True)
    cnt = cnt0 + jnp.sum(oh1, axis=1, keepdims=True)
    rank0 = jnp.sum(oh0 * c0, axis=0, keepdims=True)
    rank1 = jnp.sum(oh1 * (c1 + cnt0), axis=0, keepdims=True)
    units = jnp.floor((cnt + (MOE_ALIGN - 1)) * (1.0 / MOE_ALIGN))
    lower = jnp.where(lax.broadcasted_iota(jnp.int32, (N_EXPERTS, N_EXPERTS), 1)
                      < lax.broadcasted_iota(jnp.int32, (N_EXPERTS, N_EXPERTS), 0), 1.0, 0.0).astype(BF16)
    loc = jnp.dot(lower, jnp.broadcast_to(units, (N_EXPERTS, tile)).astype(BF16),
                  preferred_element_type=F32) * MOE_ALIGN
    pos0 = jnp.sum(oh0 * loc, axis=0, keepdims=True) + rank0
    pos1 = jnp.sum(oh1 * loc, axis=0, keepdims=True) + rank1
    fields = [pos0, pos1, w0, w1, e0.astype(F32), e1.astype(F32), rank0, rank1]
    frow = lax.broadcasted_iota(jnp.int32, (len(fields), tile), 0)
    info = jnp.zeros((len(fields), tile), F32)
    for k, field in enumerate(fields):
        info = jnp.where(frow == k, field, info)
    info_ref[0] = info
    cnt_ref[0] = jnp.broadcast_to(cnt, (N_EXPERTS, 128))


def _route(logits_t):
    n_tiles = logits_t.shape[0]
    return pl.pallas_call(
        _route_kernel,
        out_shape=[jax.ShapeDtypeStruct((n_tiles, 8, ROW_TILE), F32),
                   jax.ShapeDtypeStruct((n_tiles, N_EXPERTS, 128), F32)],
        grid=(n_tiles,),
        in_specs=[pl.BlockSpec((1, ROUTER_ROWS, ROW_TILE), lambda t: (t, 0, 0))],
        out_specs=[pl.BlockSpec((1, 8, ROW_TILE), lambda t: (t, 0, 0)),
                   pl.BlockSpec((1, N_EXPERTS, 128), lambda t: (t, 0, 0))],
        compiler_params=_cparams("arbitrary"),
        name="moe_route",
    )(logits_t)


def _dispatch_kernel(dst_ref, loc_ref, nch_ref, zfill_ref, info_ref, h2_ref, out_ref,
                     xp_ref, zero_ref, sem, zsem):
    t = pl.program_id(0)
    nt = pl.num_programs(0)
    slot = t % 2
    info = info_ref[0]
    p = lax.broadcasted_iota(jnp.int32, (MOE_TILE_SLOTS, 1), 0).astype(F32)
    sel = jnp.where(p == info[0:1], 1.0, jnp.where(p == info[1:2], 1.0, 0.0)).astype(BF16)
    xp_ref[slot] = _pack_pairs(jnp.dot(sel, h2_ref[...].astype(BF16), preferred_element_type=F32))

    def copies(tt, sl, start):
        def per_expert(e, carry):
            base = tt * N_EXPERTS + e
            loc = loc_ref[base]
            dst = dst_ref[base]

            def per_chunk(j, carry2):
                cp = pltpu.make_async_copy(
                    xp_ref.at[sl, pl.ds(pl.multiple_of(loc + j * MOE_CAP, MOE_ALIGN), MOE_CAP)],
                    out_ref.at[pl.ds(pl.multiple_of(dst + j * MOE_CAP, MOE_ALIGN), MOE_CAP)],
                    sem.at[sl])
                if start:
                    cp.start()
                else:
                    cp.wait()
                return carry2

            return lax.fori_loop(0, nch_ref[base], per_chunk, carry)

        lax.fori_loop(0, N_EXPERTS, per_expert, 0)

    @pl.when(t > 0)
    def _():
        copies(t - 1, 1 - slot, False)

    @pl.when(t == 0)
    def _():
        zero_ref[...] = jnp.zeros_like(zero_ref)

        def zero_blocks(start):
            def per_block(j, carry):
                @pl.when(zfill_ref[j] == 1)
                def _():
                    cp = pltpu.make_async_copy(
                        zero_ref, out_ref.at[pl.ds(pl.multiple_of(j * MOE_BLOCK, MOE_BLOCK), MOE_BLOCK)], zsem)
                    if start:
                        cp.start()
                    else:
                        cp.wait()

                return carry

            lax.fori_loop(0, out_ref.shape[0] // MOE_BLOCK, per_block, 0)

        zero_blocks(True)
        zero_blocks(False)

    copies(t, slot, True)

    @pl.when(t == nt - 1)
    def _():
        copies(t, slot, False)


def _dispatch(tables, info, h2_flat, n_rows_static):
    n_tiles = info.shape[0]
    grid_spec = pltpu.PrefetchScalarGridSpec(
        num_scalar_prefetch=4,
        grid=(n_tiles,),
        in_specs=[pl.BlockSpec((1, 8, ROW_TILE), lambda t, *_: (t, 0, 0)),
                  pl.BlockSpec((ROW_TILE, D_MODEL), lambda t, *_: (t, 0))],
        out_specs=pl.BlockSpec(memory_space=pl.ANY),
        scratch_shapes=[pltpu.VMEM((2, MOE_TILE_SLOTS, D_MODEL // 2), jnp.uint32),
                        pltpu.VMEM((MOE_BLOCK, D_MODEL // 2), jnp.uint32),
                        pltpu.SemaphoreType.DMA((2,)),
                        pltpu.SemaphoreType.DMA(())],
    )
    return pl.pallas_call(
        _dispatch_kernel,
        out_shape=jax.ShapeDtypeStruct((n_rows_static, D_MODEL // 2), jnp.uint32),
        grid_spec=grid_spec,
        compiler_params=_cparams("arbitrary"),
        name="moe_dispatch",
    )(tables["dst"], tables["loc"], tables["nch"], tables["zfill"], info, h2_flat)


def _ffn_kernel(bexp_ref, used_ref, xblk_ref, first_ref, x_ref, wg_ref, wu_ref, wd_ref, o_ref,
                wg_bf, wu_bf, wd_bf):
    del bexp_ref, xblk_ref
    j = pl.program_id(0)

    @pl.when(first_ref[j] == 1)
    def _():
        wg_bf[...] = wg_ref[0, 0].astype(BF16)
        wu_bf[...] = wu_ref[0, 0].astype(BF16)
        wd_bf[...] = wd_ref[0, 0].astype(BF16)

    @pl.when(used_ref[j] == 1)
    def _():
        x = _unpack_pairs(x_ref[...]).astype(BF16)
        gate = jnp.dot(x, wg_bf[...], preferred_element_type=F32)
        up = jnp.dot(x, wu_bf[...], preferred_element_type=F32)
        hid = (gate * jax.nn.sigmoid(gate) * up).astype(BF16)
        y = jnp.dot(hid, wd_bf[...], preferred_element_type=F32)
        o_ref[...] = _pack_pairs(y.astype(BF16).astype(F32))

    @pl.when(used_ref[j] == 0)
    def _():
        o_ref[...] = jnp.zeros_like(o_ref)


def _expert_ffn(tables, xs_sorted, w_gate, w_up, w_down, layer):
    P = xs_sorted.shape[0]
    w_in_spec = pl.BlockSpec((1, 1, D_MODEL, EXPERT_HIDDEN), lambda j, be, *_: (layer, be[j], 0, 0))
    grid_spec = pltpu.PrefetchScalarGridSpec(
        num_scalar_prefetch=4,
        grid=(P // MOE_BLOCK,),
        in_specs=[
            pl.BlockSpec((MOE_BLOCK, D_MODEL // 2), lambda j, be, us, xb, fi: (xb[j], 0)),
            w_in_spec, w_in_spec,
            pl.BlockSpec((1, 1, EXPERT_HIDDEN, D_MODEL), lambda j, be, *_: (layer, be[j], 0, 0)),
        ],
        out_specs=pl.BlockSpec((MOE_BLOCK, D_MODEL // 2), lambda j, *_: (j, 0)),
        scratch_shapes=[pltpu.VMEM((D_MODEL, EXPERT_HIDDEN), BF16),
                        pltpu.VMEM((D_MODEL, EXPERT_HIDDEN), BF16),
                        pltpu.VMEM((EXPERT_HIDDEN, D_MODEL), BF16)],
    )
    return pl.pallas_call(
        _ffn_kernel,
        out_shape=jax.ShapeDtypeStruct((P, D_MODEL // 2), jnp.uint32),
        grid_spec=grid_spec,
        compiler_params=_cparams("arbitrary"),
        name="expert_ffn",
    )(tables["bexp"], tables["used"], tables["xblk"], tables["first"], xs_sorted, w_gate, w_up, w_down)


def _combine_kernel(dst_ref, nch_ref, nround_ref, infoc_ref, x_ref, mod_ref, ys_ref, o_ref,
                    yp_ref, acc_ref, sem):
    nt = pl.num_programs(1)
    tile = pl.program_id(0) * nt + pl.program_id(1)
    n_tiles = pl.num_programs(0) * nt
    slot = tile % MOE_STAGE_SLOTS

    def stage(tt, sl, rnd, start):
        def per_expert(e, carry):
            base = tt * N_EXPERTS + e

            @pl.when(nch_ref[base] > rnd)
            def _():
                cp = pltpu.make_async_copy(
                    ys_ref.at[pl.ds(pl.multiple_of(dst_ref[base] + rnd * MOE_CAP, MOE_ALIGN), MOE_CAP)],
                    yp_ref.at[sl, pl.ds(pl.multiple_of(e * MOE_CAP, MOE_ALIGN), MOE_CAP)],
                    sem.at[sl])
                if start:
                    cp.start()
                else:
                    cp.wait()

            return carry

        lax.fori_loop(0, N_EXPERTS, per_expert, 0)

    @pl.when(tile == 0)
    def _():
        yp_ref[...] = jnp.zeros_like(yp_ref)
        stage(0, 0, 0, True)

        @pl.when(1 < n_tiles)
        def _():
            stage(1, 1, 0, True)

    @pl.when(tile + 2 < n_tiles)
    def _():
        stage(tile + 2, (tile + 2) % MOE_STAGE_SLOTS, 0, True)

    stage(tile, slot, 0, False)

    infoc = infoc_ref[0]
    w0, w1 = infoc[:, 2:3], infoc[:, 3:4]
    e0, e1 = infoc[:, 4:5], infoc[:, 5:6]
    r0, r1 = infoc[:, 6:7], infoc[:, 7:8]
    lane = lax.broadcasted_iota(jnp.int32, (1, N_EXPERTS * MOE_CAP), 1).astype(F32)

    def gathered(rnd):
        yp = _unpack_pairs(yp_ref[slot]).astype(BF16)

        def target(e, r):
            rr = r - rnd * MOE_CAP
            return jnp.where((rr >= 0.0) & (rr < MOE_CAP), e * MOE_CAP + rr, -1.0)

        tgt = jnp.concatenate([target(e0, r0), target(e1, r1)], axis=0)
        sel = jnp.where(lane == tgt, 1.0, 0.0).astype(BF16)
        picked = jnp.dot(sel, yp, preferred_element_type=F32)
        return w0 * picked[:ROW_TILE] + w1 * picked[ROW_TILE:]

    acc_ref[...] = gathered(0)

    def extra_round(rnd, carry):
        stage(tile, slot, rnd, True)
        stage(tile, slot, rnd, False)
        acc_ref[...] += gathered(rnd.astype(F32))
        return carry

    lax.fori_loop(1, nround_ref[tile], extra_round, 0)
    gate2 = mod_ref[0][:, 5 * D_MODEL:6 * D_MODEL]
    o_ref[0] = x_ref[0] + gate2 * acc_ref[...]


def _combine(tables, info_cols, xs, mod, ys):
    B, N, _ = xs.shape
    nt = N // ROW_TILE
    grid_spec = pltpu.PrefetchScalarGridSpec(
        num_scalar_prefetch=3,
        grid=(B, nt),
        in_specs=[pl.BlockSpec((1, ROW_TILE, 8), lambda b, i, *_: (b * nt + i, 0, 0)),
                  pl.BlockSpec((1, ROW_TILE, D_MODEL), lambda b, i, *_: (b, i, 0)),
                  pl.BlockSpec((1, 1, 6 * D_MODEL), lambda b, i, *_: (jnp.where(i == 0, B, b), 0, 0)),
                  pl.BlockSpec(memory_space=pl.ANY)],
        out_specs=pl.BlockSpec((1, ROW_TILE, D_MODEL), lambda b, i, *_: (b, i, 0)),
        scratch_shapes=[pltpu.VMEM((MOE_STAGE_SLOTS, N_EXPERTS * MOE_CAP, D_MODEL // 2), jnp.uint32),
                        pltpu.VMEM((ROW_TILE, D_MODEL), F32),
                        pltpu.SemaphoreType.DMA((MOE_STAGE_SLOTS,))],
    )
    return pl.pallas_call(
        _combine_kernel,
        out_shape=jax.ShapeDtypeStruct((B, N, D_MODEL), F32),
        grid_spec=grid_spec,
        compiler_params=_cparams("arbitrary", "arbitrary"),
        name="moe_combine",
    )(tables["dst"], tables["nch"], tables["nround"], info_cols, xs, mod, ys)


def _moe_static_rows(n_tokens):
    n_tiles = n_tokens // ROW_TILE
    worst = n_tokens * TOP_K + n_tiles * N_EXPERTS * (MOE_ALIGN - 1) + N_EXPERTS * (2 * MOE_BLOCK - 1)
    return -(-worst // MOE_BLOCK) * MOE_BLOCK


def _moe_tables(counts, n_rows_static):
    n_tiles = counts.shape[0]
    cnt = counts.astype(jnp.int32)
    cnt_al = -(-cnt // MOE_ALIGN) * MOE_ALIGN
    seg_off = jnp.cumsum(cnt_al, axis=0) - cnt_al
    tot = jnp.sum(cnt_al, axis=0)
    nblk = -(-tot // MOE_BLOCK)
    esize = (nblk + 1) * MOE_BLOCK
    eend = jnp.cumsum(esize)
    estart = eend - esize
    loc = jnp.cumsum(cnt_al, axis=1) - cnt_al
    nch = -(-cnt_al // MOE_CAP)
    blk0 = jnp.arange(n_rows_static // MOE_BLOCK, dtype=jnp.int32) * MOE_BLOCK
    bexp = jnp.minimum(jnp.sum(blk0[:, None] >= eend[None, :], axis=1), N_EXPERTS - 1).astype(jnp.int32)
    in_expert = blk0 - jnp.sum(jnp.where(blk0[:, None] >= eend[None, :], esize[None, :], 0), axis=1)
    n_used = jnp.sum(jnp.where(jnp.arange(N_EXPERTS)[None, :] == bexp[:, None], nblk[None, :], 0), axis=1)
    used = (in_expert < n_used * MOE_BLOCK) & (blk0 < eend[-1])
    blk_id = jnp.arange(blk0.shape[0], dtype=jnp.int32)
    xblk = jnp.where(used, blk_id, jnp.maximum(blk_id - (in_expert // MOE_BLOCK - n_used) - 1, 0))
    xblk = jnp.where(blk0 < eend[-1], xblk, 0)
    i32 = lambda a: a.astype(jnp.int32)
    return {
        "dst": i32((estart[None, :] + seg_off).reshape(-1)),
        "loc": i32(loc.reshape(-1)),
        "nch": i32(nch.reshape(-1)),
        "nround": i32(jnp.maximum(jnp.max(nch, axis=1), 1)),
        "zfill": i32(~used | (in_expert == (n_used - 1) * MOE_BLOCK)),
        "first": i32(used & (in_expert == 0)),
        "bexp": bexp, "used": i32(used), "xblk": i32(xblk),
    }


def _rope_tables(n_ctx, n_lat):
    row = jnp.repeat(jnp.arange(n_lat // GRID_W, dtype=F32), GRID_W)
    col = jnp.tile(jnp.arange(GRID_W, dtype=F32), n_lat // GRID_W)
    inv = ROPE_BASE ** (-jnp.arange(0, ROPE_AXIS_DIM, 2, dtype=F32) / ROPE_AXIS_DIM)
    ang = jnp.concatenate([row[:, None] * inv, col[:, None] * inv], axis=-1)
    cos = jnp.concatenate([jnp.ones((n_ctx, ROPE_AXIS_DIM), F32), jnp.cos(ang)], axis=0)
    sin = jnp.concatenate([jnp.zeros((n_ctx, ROPE_AXIS_DIM), F32), jnp.sin(ang)], axis=0)
    return jnp.tile(cos, (1, 4)), jnp.tile(jnp.concatenate([-sin, sin], axis=1), (1, 2))


def _block_diag(w):
    g, n, _ = w.shape
    out = jnp.zeros((g * n, g * n), w.dtype)
    for i in range(g):
        out = out.at[i * n:(i + 1) * n, i * n:(i + 1) * n].set(w[i])
    return out


def kernel(x, c, ctx, c_ctx, norm1_g, norm2_g, ada_w, ada_b, w_in, w_out, pool_w, pool_scale,
           q_norm_g, k_norm_g, attn_sink, hgrn_lb, hgrn_norm_g, router_group_w, router_group_b,
           router_expert_w, router_expert_b, expert_w_gate, expert_w_up, expert_w_down):
    B, L, _ = x.shape
    C = ctx.shape[1]
    N = C + L
    depth = ada_w.shape[0]
    assert C == ROW_TILE and L % ROW_TILE == 0 and L % GRID_W == 0

    cos2, sin2 = _rope_tables(C, L)
    sm = jax.nn.softmax(hgrn_lb.astype(F32), axis=0)
    lb_all = jnp.cumsum(sm, axis=0) - sm[0]

    craw = jnp.concatenate([c, c_ctx[None], jnp.zeros((8 - B - 1, D_MODEL), F32)], axis=0)
    mod_all = _ada(craw, ada_w, ada_b)[:, :B + 1].reshape(depth, B + 1, 1, 6 * D_MODEL)

    xs = jnp.concatenate([ctx, x], axis=1)
    for l in range(depth):
        mod = mod_all[l]
        u, q, k, v, hq, hi, hzf, hzb, hgate = _in_proj(xs, norm1_g[l], mod, w_in[l].astype(BF16))
        a = _pool(u, _block_diag(pool_w[l]).astype(BF16), pool_scale[l], C)
        qh, kh, vh = _qk_prep(q, k, v, cos2, sin2,
                              jnp.tile(q_norm_g[l].reshape(1, HEAD_DIM), (1, 2)),
                              jnp.tile(k_norm_g[l].reshape(1, HEAD_DIM), (1, 2)))
        b_mix = _attention(qh, kh, vh, attn_sink[l], C)
        o_f = _hgrn_scan(hq, hi, hzf, lb_all[l, 0], reverse=False)
        c_mix = _hgrn_scan(hq, hi, hzb, lb_all[l, 1], reverse=True, o_fwd=o_f, hgate=hgate,
                           norm_g=hgrn_norm_g[l])
        pad_cols = lambda n: jnp.zeros((D_MODEL, n), F32)
        r_w = jnp.concatenate([router_group_w[l], pad_cols(ROUTER_EXPERT_ROW0 - N_GROUPS),
                               router_expert_w[l],
                               pad_cols(ROUTER_ROWS - ROUTER_EXPERT_ROW0 - N_EXPERTS)], axis=1)
        r_b = jnp.concatenate([router_group_b[l], jnp.zeros((ROUTER_EXPERT_ROW0 - N_GROUPS,), F32),
                               router_expert_b[l],
                               jnp.zeros((ROUTER_ROWS - ROUTER_EXPERT_ROW0 - N_EXPERTS,), F32)])
        r_hi = r_w.astype(BF16)
        r_lo = (r_w - r_hi.astype(F32)).astype(BF16)
        xs, h2, logits_t = _out_proj(a, b_mix, c_mix, xs, w_out[l].astype(BF16), mod, norm2_g[l],
                                     r_hi, r_lo, r_b.reshape(1, ROUTER_ROWS))
        info, counts = _route(logits_t)
        n_rows_static = _moe_static_rows(B * N)
        tables = _moe_tables(counts[:, :, 0], n_rows_static)
        xs_sorted = _dispatch(tables, info, h2.reshape(B * N, D_MODEL), n_rows_static)
        ys = _expert_ffn(tables, xs_sorted, expert_w_gate, expert_w_up, expert_w_down, l)
        xs = _combine(tables, jnp.swapaxes(info, 1, 2), xs, mod, ys)
    return xs[:, C:]
```
